```python
import jax, jax.numpy as jnp
from jax import lax
import numpy as np

D_MODEL = 1024
BATCH = 2
SEQ = 8192
DEPTH = 4

POOL_WINDOWS = (2, 4, 8, 16)
POOL_GROUPS = 4
POOL_W = D_MODEL // 2
POOL_GC = POOL_W // POOL_GROUPS
HEAD_DIM = 64
B_HEADS = 8
B_W = B_HEADS * HEAD_DIM
IDX_HEADS = 8
IDX_DIM = 32
DSA_TOPK = 256
C_HEADS = 8
C_KV_GROUPS = 2
C_W = C_HEADS * HEAD_DIM
KVW = C_KV_GROUPS * HEAD_DIM
CMP_LEN = 32
CMP_STRIDE = 16
SLC_LEN = 64
SLC_N = 16
WIN = 512
FORCE_BONUS = 1e4
N_BRANCH = 3
QBLK = 128
LN_EPS = 1e-5
NEG = -1e30
ALPHA = (2 * DEPTH) ** 0.25
BETA = (8 * DEPTH) ** -0.25

IN_WIDTHS = (POOL_W, POOL_W,
             B_W, HEAD_DIM, HEAD_DIM, B_W,
             IDX_HEADS * IDX_DIM, IDX_DIM, IDX_HEADS,
             C_W, KVW, KVW, KVW, KVW, KVW, KVW,
             C_HEADS * 3, C_W,
             N_BRANCH * D_MODEL)
N_IN = sum(IN_WIDTHS)

kernel_name = "hybrid_pool_dsa_nsa_deepnorm"


def _layernorm(x, g, b):
    xf = x.astype(jnp.float32)
    mu = jnp.mean(xf, axis=-1, keepdims=True)
    var = jnp.mean(jnp.square(xf - mu), axis=-1, keepdims=True)
    return ((xf - mu) * lax.rsqrt(var + LN_EPS) * g + b).astype(x.dtype)


def _masked_softmax(s, mask):
    p = jax.nn.softmax(jnp.where(mask, s.astype(jnp.float32), NEG), axis=-1)
    return p * mask


def _pool_mixer(xa, w, b, scale):
    B_, S, _ = xa.shape
    xg = xa.reshape(B_, S, POOL_GROUPS, POOL_GC)
    c = jnp.pad(jnp.cumsum(xg.astype(jnp.float32), axis=1), ((0, 0), (1, 0), (0, 0), (0, 0)))
    pos = jnp.arange(1, S + 1, dtype=jnp.float32)
    outs = []
    for g, wnd in enumerate(POOL_WINDOWS):
        cg = c[:, :, g]
        lo = jnp.pad(cg[:, :S + 1 - wnd], ((0, 0), (wnd - 1, 0), (0, 0)))
        mean = (cg[:, 1:] - lo) / jnp.minimum(pos, float(wnd))[None, :, None]
        outs.append(mean - xg[:, :, g].astype(jnp.float32))
    pooled = jnp.stack(outs, axis=2).astype(xa.dtype)
    y = jnp.einsum('bsgc,gcd->bsgd', pooled, w) + b
    return y.reshape(B_, S, POOL_W) * scale


def _dsa_mixer(q, k, v, iq, ik, iw):
    B_, S = q.shape[:2]
    topk = min(DSA_TOPK, S // 4)
    nb = S // QBLK
    key_pos = jnp.arange(S)
    gather = jax.vmap(lambda kk, ii: kk[ii])

    def block(i):
        q0 = i * QBLK
        t = q0 + jnp.arange(QBLK)
        qb = lax.dynamic_slice_in_dim(q, q0, QBLK, axis=1)
        iqb = lax.dynamic_slice_in_dim(iq, q0, QBLK, axis=1)
        iwb = lax.dynamic_slice_in_dim(iw, q0, QBLK, axis=1)
        rel = jax.nn.relu(jnp.einsum('bthd,bsd->bths', iqb, ik))
        score = jnp.einsum('bths,bth->bts', rel, iwb).astype(jnp.float32)
        causal = key_pos[None, :] <= t[:, None]
        score = jnp.where(causal[None], score, NEG)
        _, idx = lax.top_k(score, topk)
        kg = gather(k, idx)
        vg = gather(v, idx)
        s = jnp.einsum('bthd,btkd->bthk', qb, kg) * (HEAD_DIM ** -0.5)
        valid = (idx <= t[None, :, None])[:, :, None, :]
        p = _masked_softmax(s, valid).astype(v.dtype)
        return jnp.einsum('bthk,btkd->bthd', p, vg)

    o = lax.map(block, jnp.arange(nb))
    return o.transpose(1, 0, 2, 3, 4).reshape(B_, S, B_W)


def _nsa_mixer(q, kc_tok, vc_tok, ks, vs, kw, vw, gates, pos_k, pos_v, w1k, w2k, w1v, w2v):
    B_, S = q.shape[:2]
    G, R, dh = C_KV_GROUPS, C_HEADS // C_KV_GROUPS, HEAD_DIM
    nb = S // QBLK
    n_c = (S - CMP_LEN) // CMP_STRIDE + 1
    n_s = S // SLC_LEN
    n_sel = min(SLC_N, n_s)
    nk_sel = n_sel * SLC_LEN
    c_start = jnp.arange(n_c) * CMP_STRIDE
    blk_idx = c_start[:, None] + jnp.arange(CMP_LEN)[None, :]

    def compress(tok, pos, w1, w2):
        blk = tok[:, blk_idx] + pos[:, None, :]
        blk = blk.transpose(0, 1, 3, 2, 4).reshape(B_, n_c, G, CMP_LEN * dh)
        h = jax.nn.silu(jnp.einsum('bngf,fe->bnge', blk, w1))
        return jnp.einsum('bnge,ef->bngf', h, w2)

    kcmp = compress(kc_tok, pos_k, w1k, w2k)
    vcmp = compress(vc_tok, pos_v, w1v, w2v)
    cmp_end = c_start + CMP_LEN - 1
    s_start = jnp.arange(n_s) * SLC_LEN
    overlap = ((c_start[:, None] <= s_start[None, :] + SLC_LEN - 1)
               & (cmp_end[:, None] >= s_start[None, :])).astype(jnp.float32)
    ksT = ks.transpose(0, 2, 1, 3)
    vsT = vs.transpose(0, 2, 1, 3)
    kw_pad = jnp.pad(kw, ((0, 0), (WIN, 0), (0, 0), (0, 0)))
    vw_pad = jnp.pad(vw, ((0, 0), (WIN, 0), (0, 0), (0, 0)))
    gather2 = jax.vmap(jax.vmap(lambda kk, ii: kk[ii]))
    sel_j = jnp.arange(n_s)
    scale = HEAD_DIM ** -0.5

    def block(i):
        q0 = i * QBLK
        t = q0 + jnp.arange(QBLK)
        qb = lax.dynamic_slice_in_dim(q, q0, QBLK, axis=1).reshape(B_, QBLK, G, R, dh) * scale
        s_c = jnp.einsum('btgrd,bngd->btgrn', qb, kcmp)
        m_c = (cmp_end[None, :] <= t[:, None])[None, :, None, None, :]
        p_c = _masked_softmax(s_c, m_c)
        o_c = jnp.einsum('btgrn,bngd->btgrd', p_c.astype(vcmp.dtype), vcmp)
        imp = jnp.einsum('btgrn,ns->btgs', p_c, overlap)
        blk_t = t // SLC_LEN
        forced = (sel_j[None, :] == 0) | (sel_j[None, :] == blk_t[:, None]) | (sel_j[None, :] == blk_t[:, None] - 1)
        admissible = s_start[None, :] <= t[:, None]
        imp = imp + jnp.where(forced, FORCE_BONUS, 0.0)[None, :, None, :]
        imp = jnp.where(admissible[None, :, None, :], imp, NEG)
        _, sel = lax.top_k(imp, n_sel)
        tok = (sel[..., None] * SLC_LEN + jnp.arange(SLC_LEN)).reshape(B_, QBLK, G, nk_sel)
        tokT = tok.transpose(0, 2, 1, 3).reshape(B_, G, QBLK * nk_sel)
        kg = gather2(ksT, tokT).reshape(B_, G, QBLK, nk_sel, dh)
        vg = gather2(vsT, tokT).reshape(B_, G, QBLK, nk_sel, dh)
        s_s = jnp.einsum('btgrd,bgtkd->btgrk', qb, kg)
        m_s = (tok <= t[None, :, None, None])[:, :, :, None, :]
        p_s = _masked_softmax(s_s, m_s).astype(vg.dtype)
        o_s = jnp.einsum('btgrk,bgtkd->btgrd', p_s, vg)
        kwb = lax.dynamic_slice_in_dim(kw_pad, q0, QBLK + WIN, axis=1)
        vwb = lax.dynamic_slice_in_dim(vw_pad, q0, QBLK + WIN, axis=1)
        kpos = q0 - WIN + jnp.arange(QBLK + WIN)
        m_w = (kpos[None, :] >= 0) & (kpos[None, :] <= t[:, None]) & (kpos[None, :] > t[:, None] - WIN)
        s_w = jnp.einsum('btgrd,bkgd->btgrk', qb, kwb)
        p_w = _masked_softmax(s_w, m_w[None, :, None, None, :]).astype(vwb.dtype)
        o_w = jnp.einsum('btgrk,bkgd->btgrd', p_w, vwb)
        gb = jax.nn.sigmoid(lax.dynamic_slice_in_dim(gates, q0, QBLK, axis=1)).reshape(B_, QBLK, G, R, 3)
        o = gb[..., 0:1] * o_c + gb[..., 1:2] * o_s + gb[..., 2:3] * o_w
        return o.reshape(B_, QBLK, C_W)

    o = lax.map(block, jnp.arange(nb))
    return o.transpose(1, 0, 2, 3).reshape(B_, S, C_W)


def _layer(x, w_in, b_in, pool_w, pool_b, pool_scale, pos_k, pos_v, w1k, w2k, w1v, w2v,
           w_pa, w_pb, w_pc, w_o, ln_g, ln_b):
    B_, S, _ = x.shape
    u = jnp.einsum('bsd,de->bse', x, w_in) + b_in
    points = [int(p) for p in np.cumsum(IN_WIDTHS)[:-1]]
    (a_x, a_z, b_q, b_k, b_v, b_z, i_q, i_k, i_w, c_q, c_kc, c_vc, c_ks, c_vs,
     c_kw, c_vw, c_g, c_z, g_merge) = jnp.split(u, points, axis=-1)
    y_a = _pool_mixer(a_x, pool_w, pool_b, pool_scale) * jax.nn.silu(a_z)
    y_b = _dsa_mixer(b_q.reshape(B_, S, B_HEADS, HEAD_DIM), b_k, b_v,
                     i_q.reshape(B_, S, IDX_HEADS, IDX_DIM), i_k,
                     i_w * (IDX_HEADS ** -0.5 * IDX_DIM ** -0.5)) * jax.nn.silu(b_z)
    kv = lambda t_: t_.reshape(B_, S, C_KV_GROUPS, HEAD_DIM)
    y_c = _nsa_mixer(c_q.reshape(B_, S, C_HEADS, HEAD_DIM), kv(c_kc), kv(c_vc), kv(c_ks), kv(c_vs),
                     kv(c_kw), kv(c_vw), c_g.reshape(B_, S, C_HEADS, 3),
                     pos_k, pos_v, w1k, w2k, w1v, w2v) * jax.nn.silu(c_z)
    g = jax.nn.sigmoid(g_merge).reshape(B_, S, N_BRANCH, D_MODEL)
    m = (g[:, :, 0] * jnp.einsum('bse,ed->bsd', y_a, w_pa)
         + g[:, :, 1] * jnp.einsum('bse,ed->bsd', y_b, w_pb)
         + g[:, :, 2] * jnp.einsum('bse,ed->bsd', y_c, w_pc))
    out = jnp.einsum('bsd,de->bse', m, w_o)
    return _layernorm(ALPHA * x + out, ln_g, ln_b)


def setup_inputs(seed: int = 0) -> dict:
    key = jax.random.key(seed)
    ks = jax.random.split(key, 19)
    nrm = lambda k, shape, s: jax.random.normal(k, shape, jnp.float32) * s
    L, D = DEPTH, D_MODEL
    return {
        "x": nrm(ks[0], (BATCH, SEQ, D), 1.0),
        "w_in": nrm(ks[1], (L, D, N_IN), D ** -0.5),
        "b_in": nrm(ks[2], (L, N_IN), 0.02),
        "pool_w": nrm(ks[3], (L, POOL_GROUPS, POOL_GC, POOL_GC), POOL_GC ** -0.5),
        "pool_b": nrm(ks[4], (L, POOL_GROUPS, POOL_GC), 0.02),
        "pool_scale": 1.0 + nrm(ks[5], (L, POOL_W), 0.02),
        "cmp_pos_k": nrm(ks[6], (L, CMP_LEN, HEAD_DIM), 0.1),
        "cmp_pos_v": nrm(ks[7], (L, CMP_LEN, HEAD_DIM), 0.1),
        "cmp_w1_k": nrm(ks[8], (L, CMP_LEN * HEAD_DIM, HEAD_DIM), (CMP_LEN * HEAD_DIM) ** -0.5),
        "cmp_w2_k": nrm(ks[9], (L, HEAD_DIM, HEAD_DIM), HEAD_DIM ** -0.5),
        "cmp_w1_v": nrm(ks[10], (L, CMP_LEN * HEAD_DIM, HEAD_DIM), (CMP_LEN * HEAD_DIM) ** -0.5),
        "cmp_w2_v": nrm(ks[11], (L, HEAD_DIM, HEAD_DIM), HEAD_DIM ** -0.5),
        "w_proj_a": nrm(ks[12], (L, POOL_W, D), POOL_W ** -0.5 * BETA),
        "w_proj_b": nrm(ks[13], (L, B_W, D), B_W ** -0.5 * BETA),
        "w_proj_c": nrm(ks[14], (L, C_W, D), C_W ** -0.5 * BETA),
        "w_o": nrm(ks[15], (L, D, D), D ** -0.5 * BETA),
        "ln_g": 1.0 + nrm(ks[16], (L, D), 0.02),
        "ln_b": nrm(ks[17], (L, D), 0.02),
    }


def reference(x, w_in, b_in, pool_w, pool_b, pool_scale, cmp_pos_k, cmp_pos_v, cmp_w1_k, cmp_w2_k,
              cmp_w1_v, cmp_w2_v, w_proj_a, w_proj_b, w_proj_c, w_o, ln_g, ln_b):
    h = x
    for l in range(DEPTH):
        h = _layer(h, w_in[l], b_in[l], pool_w[l], pool_b[l], pool_scale[l],
                   cmp_pos_k[l], cmp_pos_v[l], cmp_w1_k[l], cmp_w2_k[l], cmp_w1_v[l], cmp_w2_v[l],
                   w_proj_a[l], w_proj_b[l], w_proj_c[l], w_o[l], ln_g[l], ln_b[l])
    return h
```

```python
import functools

import numpy as np
import jax
import jax.numpy as jnp
from jax import lax
from jax.experimental import pallas as pl
from jax.experimental.pallas import tpu as pltpu

F32 = jnp.float32
I32 = jnp.int32
_MXU_DTYPE = jnp.bfloat16

D_MODEL = 1024
HEAD_DIM = 64
LANES = 128
POOL_WINDOWS = (2, 4, 8, 16)
POOL_GC = 128
N_HEADS = 8
IDX_HEADS = 8
IDX_DIM = 32
DSA_TOPK = 256
C_KV_GROUPS = 2
HEADS_PER_GROUP = N_HEADS // C_KV_GROUPS
CMP_LEN = 32
CMP_STRIDE = 16
SLC_LEN = 64
SLC_N = 16
WIN = 512
FORCE_BONUS = 1e4
LN_EPS = 1e-5
NEG = -1e30
QBLK = 128
KCHUNK = 512
INT_MIN = -2 ** 31
VMEM_LIMIT = 56 * 1024 * 1024

_IN_WIDTHS = (512, 512, 512, 64, 64, 512, 256, 32, 8, 512, 128, 128, 128, 128, 128, 128, 24, 512, 3072)
_IN_NAMES = ("a_x", "a_z", "b_q", "b_k", "b_v", "b_z", "i_q", "i_k", "i_w", "c_q", "c_kc", "c_vc",
             "c_ks", "c_vs", "c_kw", "c_vw", "c_g", "c_z", "g_merge")
_N_IN = sum(_IN_WIDTHS)
_OFF = dict(zip(_IN_NAMES, np.cumsum((0,) + _IN_WIDTHS[:-1])))
_WID = dict(zip(_IN_NAMES, _IN_WIDTHS))


def _seg(name, lo=0, hi=None):
    hi = _WID[name] if hi is None else hi
    return np.arange(_OFF[name] + lo, _OFF[name] + hi)


def _pad(n):
    return np.full((n,), _N_IN)


def _layout32():
    segs, off, pos = [], {}, 0

    def add(name, idx):
        nonlocal pos
        off[name] = pos
        segs.append(idx)
        pos += len(idx)

    add("a_x", _seg("a_x"))
    add("a_z", _seg("a_z"))
    add("b_z", _seg("b_z"))
    add("c_z", _seg("c_z"))
    add("g_merge", _seg("g_merge"))
    add("c_kc", _seg("c_kc"))
    add("c_vc", _seg("c_vc"))
    add("i_w", np.concatenate([_seg("i_w"), _pad(LANES - 8)]))
    for g in range(C_KV_GROUPS):
        add(f"c_g{g}", np.concatenate([_seg("c_g", 12 * g, 12 * g + 12), _pad(LANES - 12)]))
    idx = np.concatenate(segs)
    return idx, np.ones((len(idx),), np.float32), off


def _layout16():
    segs, scales, off, pos = [], [], {}, 0

    def add(name, idx, scale=1.0):
        nonlocal pos
        off[name] = pos
        segs.append(idx)
        scales.append(np.full((len(idx),), scale, np.float32))
        pos += len(idx)

    qk_scale = HEAD_DIM ** -0.5
    add("b_q", _seg("b_q"), qk_scale)
    add("c_q", _seg("c_q"), qk_scale)
    add("i_q", _seg("i_q"))
    for name in ("c_ks", "c_vs", "c_kw", "c_vw"):
        for g in range(C_KV_GROUPS):
            one = _seg(name, HEAD_DIM * g, HEAD_DIM * (g + 1))
            add(f"{name}{g}", np.concatenate([one, one]))
    add("b_k", np.concatenate([_seg("b_k")] * 2))
    add("b_v", np.concatenate([_seg("b_v")] * 2))
    add("i_k", np.concatenate([_seg("i_k")] * (LANES // IDX_DIM)))
    return np.concatenate(segs), np.concatenate(scales), off


_IDX32, _SCALE32, _OFF32 = _layout32()
_IDX16, _SCALE16, _OFF16 = _layout16()
_N32 = len(_IDX32)
_N16 = len(_IDX16)
_TN32 = 640
_TN16 = 384
assert _N32 % _TN32 == 0 and _N16 % _TN16 == 0


def _sigmoid(x):
    return 1.0 / (1.0 + jnp.exp(-x))


def _dot(a, b):
    return jnp.dot(a, b, preferred_element_type=F32)


def _dot_nt(a, b):
    return lax.dot_general(a, b, (((1,), (1,)), ((), ())), preferred_element_type=F32)


def _mm_bias_kernel(x_ref, w_ref, b_ref, o_ref):
    o_ref[...] = (_dot(x_ref[...], w_ref[...]) + b_ref[...]).astype(o_ref.dtype)


def _matmul_bias(x, w, b, out_dtype, tm, tn, name):
    m, k = x.shape
    n = w.shape[1]
    return pl.pallas_call(
        _mm_bias_kernel,
        grid=(m // tm, n // tn),
        in_specs=[pl.BlockSpec((tm, k), lambda i, j: (i, 0)),
                  pl.BlockSpec((k, tn), lambda i, j: (0, j)),
                  pl.BlockSpec((1, tn), lambda i, j: (0, j))],
        out_specs=pl.BlockSpec((tm, tn), lambda i, j: (i, j)),
        out_shape=jax.ShapeDtypeStruct((m, n), out_dtype),
        compiler_params=pltpu.CompilerParams(dimension_semantics=("arbitrary", "arbitrary"),
                                             vmem_limit_bytes=VMEM_LIMIT),
        name=name,
    )(x, w, b)


_HALO = 16


def _pool_kernel(xa_ref, halo_ref, az_ref, pw_ref, pb_ref, ps_ref, o_ref, *, tb):
    i = pl.program_id(1)
    cur = xa_ref[0]
    halo = jnp.where(i > 0, halo_ref[0], 0.0)
    ext = jnp.concatenate([halo, cur], axis=0)
    pos = (i * tb + 1 + lax.broadcasted_iota(I32, (tb, 1), 0)).astype(F32)
    outs = []
    for g, wnd in enumerate(POOL_WINDOWS):
        s = ext[:, g * POOL_GC:(g + 1) * POOL_GC]
        k = 1
        while k < wnd:
            s = s + pltpu.roll(s, k, axis=0)
            k *= 2
        mean = s[_HALO:] / jnp.minimum(pos, float(wnd))
        pooled = mean - cur[:, g * POOL_GC:(g + 1) * POOL_GC]
        outs.append(_dot(pooled.astype(_MXU_DTYPE), pw_ref[g]))
    y = jnp.concatenate(outs, axis=1) + pb_ref[...]
    az = az_ref[0]
    o_ref[0] = (y * ps_ref[...] * (az * _sigmoid(az))).astype(o_ref.dtype)


def _pool_mixer(u32, pool_w, pool_b, pool_scale, tb=512):
    b, s, _ = u32.shape
    hb = tb // _HALO
    return pl.pallas_call(
        functools.partial(_pool_kernel, tb=tb),
        grid=(b, s // tb),
        in_specs=[pl.BlockSpec((1, tb, 512), lambda bi, i: (bi, i, _OFF32["a_x"] // 512)),
                  pl.BlockSpec((1, _HALO, 512), lambda bi, i: (bi, jnp.maximum(i * hb - 1, 0), _OFF32["a_x"] // 512)),
                  pl.BlockSpec((1, tb, 512), lambda bi, i: (bi, i, _OFF32["a_z"] // 512)),
                  pl.BlockSpec((4, POOL_GC, POOL_GC), lambda bi, i: (0, 0, 0)),
                  pl.BlockSpec((1, 512), lambda bi, i: (0, 0)),
                  pl.BlockSpec((1, 512), lambda bi, i: (0, 0))],
        out_specs=pl.BlockSpec((1, tb, 512), lambda bi, i: (bi, i, 0)),
        out_shape=jax.ShapeDtypeStruct((b, s, 512), _MXU_DTYPE),
        compiler_params=pltpu.CompilerParams(dimension_semantics=("arbitrary", "arbitrary"),
                                             vmem_limit_bytes=VMEM_LIMIT),
        name="pool_mixer",
    )(u32, u32, u32, pool_w, pool_b, pool_scale)


def _compress_kernel(ch_ref, pos_ref, w1_ref, w2_ref, o_ref):
    ch = ch_ref[0, 0]
    pos = pos_ref[0]
    w1 = w1_ref[0]
    half = ch.shape[1]
    n = ch.shape[0]
    a = _dot((ch + pos[0:1]).astype(_MXU_DTYPE), w1[:half])
    bb = _dot((ch + pos[1:2]).astype(_MXU_DTYPE), w1[half:])
    h = a + pltpu.roll(bb, n - 1, axis=0)
    h = h * _sigmoid(h)
    o_ref[0, 0] = _dot(h.astype(_MXU_DTYPE), w2_ref[0]).astype(o_ref.dtype)


def _compress(chunks, pos, w1, w2dup):
    _, bg, n, width = chunks.shape
    return pl.pallas_call(
        _compress_kernel,
        grid=(2, bg),
        in_specs=[pl.BlockSpec((1, 1, n, width), lambda kv, i: (kv, i, 0, 0)),
                  pl.BlockSpec((1, 2, width), lambda kv, i: (kv, 0, 0)),
                  pl.BlockSpec((1, 2 * width, HEAD_DIM), lambda kv, i: (kv, 0, 0)),
                  pl.BlockSpec((1, HEAD_DIM, LANES), lambda kv, i: (kv, 0, 0))],
        out_specs=pl.BlockSpec((1, 1, n, LANES), lambda kv, i: (kv, i, 0, 0)),
        out_shape=jax.ShapeDtypeStruct((2, bg, n, LANES), _MXU_DTYPE),
        compiler_params=pltpu.CompilerParams(dimension_semantics=("arbitrary", "arbitrary"),
                                             vmem_limit_bytes=VMEM_LIMIT),
        name="nsa_compress",
    )(chunks, pos, w1, w2dup)


def _stack_heads(qf, n_tiles, width):
    lane = lax.broadcasted_iota(I32, (1, LANES), 1)
    per_tile = LANES // width
    parts = []
    for tix in range(n_tiles):
        qt = qf[:, tix * LANES:(tix + 1) * LANES]
        for j in range(per_tile):
            keep = (lane >= j * width) & (lane < (j + 1) * width)
            parts.append(jnp.where(keep, qt, 0.0))
    return jnp.concatenate(parts, axis=0).astype(_MXU_DTYPE)


def _flash(qs, k_ref, v_ref, bias_fn, nchunks, ck):
    rows = qs.shape[0]
    reps = rows // QBLK

    def body(c, carry):
        m, l, acc = carry
        off = pl.multiple_of(c * ck, ck)
        kc = k_ref[pl.ds(off, ck), :]
        vc = v_ref[pl.ds(off, ck), :]
        s = _dot_nt(qs, kc) + jnp.concatenate([bias_fn(c)] * reps, axis=0)
        m_new = jnp.maximum(m, jnp.max(s, axis=1, keepdims=True))
        a = jnp.exp(m - m_new)
        p = jnp.exp(s - m_new)
        l = a * l + jnp.sum(p, axis=1, keepdims=True)
        acc = a * acc + _dot(p.astype(_MXU_DTYPE), vc)
        return m_new, l, acc

    m0 = jnp.full((rows, 1), -1e38, F32)
    l0 = jnp.zeros((rows, 1), F32)
    a0 = jnp.zeros((rows, LANES), F32)
    _, l, acc = lax.fori_loop(0, nchunks, body, (m0, l0, a0))
    return acc / l


def _masked_softmax_pv(s, mask, v):
    s = jnp.where(mask, s, NEG)
    m = jnp.max(s, axis=1, keepdims=True)
    p = jnp.exp(s - m)
    p = jnp.where(mask, p / jnp.sum(p, axis=1, keepdims=True), 0.0).astype(_MXU_DTYPE)
    return p, _dot(p, v)


def _pair_heads(heads):
    lo = lax.broadcasted_iota(I32, (1, LANES), 1) < HEAD_DIM
    tiles = [jnp.where(lo, heads[i], heads[i + 1]) for i in range(0, len(heads), 2)]
    return jnp.concatenate(tiles, axis=1)


def _nsa_kernel(q_ref, kc_ref, vc_ref, ks_ref, vs_ref, kw_ref, vw_ref, cg_ref, cz_ref, ov_ref, e_ref, o_ref,
                *, ck, win, n_sel):
    qi = pl.program_id(2)
    q0 = qi * QBLK
    hpg = HEADS_PER_GROUP
    t = q0 + lax.broadcasted_iota(I32, (QBLK, 1), 0)
    t4 = jnp.concatenate([t] * hpg, axis=0)
    qs = _stack_heads(q_ref[0].astype(F32), hpg // 2, HEAD_DIM)

    ncp = kc_ref.shape[2]
    cend = lax.broadcasted_iota(I32, (1, ncp), 1) * CMP_STRIDE + (CMP_LEN - 1)
    p_c, o_c = _masked_softmax_pv(_dot_nt(qs, kc_ref[0, 0]), cend <= t4, vc_ref[0, 0])

    imp4 = _dot(p_c, ov_ref[...])
    imp = imp4[0:QBLK]
    for r in range(1, hpg):
        imp = imp + imp4[r * QBLK:(r + 1) * QBLK]
    n_s = imp.shape[1]
    work = imp.T
    ji = lax.broadcasted_iota(I32, (n_s, 1), 0)
    jf = ji.astype(F32)
    tq = q0 + lax.broadcasted_iota(I32, (1, QBLK), 1)
    blk = lax.shift_right_logical(tq, 6)
    forced = (ji == 0) | (ji == blk) | (ji == blk - 1)
    work = work + jnp.where(forced, FORCE_BONUS, 0.0)
    work = jnp.where(ji * SLC_LEN <= tq, work, NEG)
    sel = jnp.zeros((n_s, QBLK), F32)
    for _ in range(n_sel):
        mx = jnp.max(work, axis=0, keepdims=True)
        first = jnp.min(jnp.where(work == mx, jf, float(n_s)), axis=0, keepdims=True)
        hit = jf == first
        sel = jnp.where(hit, 1.0, sel)
        work = jnp.where(hit, -3e38, work)
    selb = jnp.where(sel.T > 0.5, 0.0, NEG).astype(_MXU_DTYPE)

    def bias_fn(c):
        kpos = c * ck + lax.broadcasted_iota(I32, (1, ck), 1)
        return jnp.where(kpos <= t, _dot(selb, e_ref[c]), NEG)

    o_s = _flash(qs, ks_ref.at[0], vs_ref.at[0], bias_fn, q0 // ck + 1, ck)

    span = win + QBLK
    start = pl.multiple_of(jnp.maximum(q0 - win, 0), QBLK)
    kpos = start + lax.broadcasted_iota(I32, (1, span), 1)
    mask_w = (kpos <= t4) & (kpos > t4 - win)
    _, o_w = _masked_softmax_pv(_dot_nt(qs, kw_ref[0, pl.ds(start, span), :]), mask_w,
                                vw_ref[0, pl.ds(start, span), :])

    gb = _sigmoid(cg_ref[0][:, 0:3 * hpg])
    heads = []
    for r in range(hpg):
        rows = slice(r * QBLK, (r + 1) * QBLK)
        heads.append(gb[:, 3 * r:3 * r + 1] * o_c[rows] + gb[:, 3 * r + 1:3 * r + 2] * o_s[rows]
                     + gb[:, 3 * r + 2:3 * r + 3] * o_w[rows])
    cz = cz_ref[0]
    o_ref[0] = (_pair_heads(heads) * (cz * _sigmoid(cz))).astype(o_ref.dtype)


def _nsa_mixer(u16, u32, kvcmp, overlap, expand):
    b, s, _ = u16.shape
    ck = min(KCHUNK, s)
    gw = HEADS_PER_GROUP * HEAD_DIM
    ncp = kvcmp.shape[2]

    def kv_spec(name):
        base = _OFF16[name + "0"] // LANES
        return pl.BlockSpec((1, s, LANES), lambda bi, g, qi: (bi, 0, base + g))

    return pl.pallas_call(
        functools.partial(_nsa_kernel, ck=ck, win=WIN, n_sel=min(SLC_N, s // SLC_LEN)),
        grid=(b, C_KV_GROUPS, s // QBLK),
        in_specs=[pl.BlockSpec((1, QBLK, gw), lambda bi, g, qi: (bi, qi, _OFF16["c_q"] // gw + g)),
                  pl.BlockSpec((1, 1, ncp, LANES), lambda bi, g, qi: (0, bi * C_KV_GROUPS + g, 0, 0)),
                  pl.BlockSpec((1, 1, ncp, LANES), lambda bi, g, qi: (1, bi * C_KV_GROUPS + g, 0, 0)),
                  kv_spec("c_ks"), kv_spec("c_vs"), kv_spec("c_kw"), kv_spec("c_vw"),
                  pl.BlockSpec((1, QBLK, LANES), lambda bi, g, qi: (bi, qi, _OFF32["c_g0"] // LANES + g)),
                  pl.BlockSpec((1, QBLK, gw), lambda bi, g, qi: (bi, qi, _OFF32["c_z"] // gw + g)),
                  pl.BlockSpec(overlap.shape, lambda bi, g, qi: (0, 0)),
                  pl.BlockSpec(expand.shape, lambda bi, g, qi: (0, 0, 0))],
        out_specs=pl.BlockSpec((1, QBLK, gw), lambda bi, g, qi: (bi, qi, g)),
        out_shape=jax.ShapeDtypeStruct((b, s, N_HEADS * HEAD_DIM), _MXU_DTYPE),
        compiler_params=pltpu.CompilerParams(dimension_semantics=("arbitrary", "arbitrary", "arbitrary"),
                                             vmem_limit_bytes=VMEM_LIMIT),
        name="nsa_mixer",
    )(u16, kvcmp, kvcmp, u16, u16, u16, u16, u32, u32, overlap, expand)


def _dsa_kernel(q_ref, iq_ref, k_ref, v_ref, ik_ref, iw_ref, bz_ref, o_ref, keys_ref, bias_ref, cut_ref,
                *, ck, topk, nbits):
    qi = pl.program_id(1)
    q0 = qi * QBLK
    nch = q0 // ck + 1
    t = q0 + lax.broadcasted_iota(I32, (QBLK, 1), 0)
    lane_pos = lax.broadcasted_iota(I32, (1, ck), 1)
    fold = ck // LANES

    iqs = _stack_heads(iq_ref[0].astype(F32), IDX_HEADS * IDX_DIM // LANES, IDX_DIM)
    iw = iw_ref[0] * (IDX_HEADS ** -0.5 * IDX_DIM ** -0.5)
    wcols = [jnp.broadcast_to(iw[:, h:h + 1], (QBLK, ck)) for h in range(IDX_HEADS)]

    def score_body(c, _):
        off = pl.multiple_of(c * ck, ck)
        rel = _dot_nt(iqs, ik_ref[0, pl.ds(off, ck), :])
        sc = jnp.maximum(rel[0:QBLK], 0.0) * wcols[0]
        for h in range(1, IDX_HEADS):
            sc = sc + jnp.maximum(rel[h * QBLK:(h + 1) * QBLK], 0.0) * wcols[h]
        sc = jnp.where(c * ck + lane_pos <= t, sc, NEG)
        sc = jnp.where(sc == 0.0, 0.0, sc)
        bits = pltpu.bitcast(sc, I32)
        keys_ref[c] = bits ^ (lax.shift_right_arithmetic(bits, 31) & 0x7FFFFFFF)
        return 0

    lax.fori_loop(0, nch, score_body, 0)

    def count(pred):
        def body(c, acc):
            ind = jnp.where(pred(keys_ref[c], c * ck + lane_pos), 1.0, 0.0)
            part = ind[:, 0:LANES]
            for i in range(1, fold):
                part = part + ind[:, i * LANES:(i + 1) * LANES]
            return acc + part
        acc = lax.fori_loop(0, nch, body, jnp.zeros((QBLK, LANES), F32))
        return jnp.sum(acc, axis=1, keepdims=True)

    def bit_body(i, thr):
        cand = thr + lax.shift_left(jnp.int32(1), 31 - i)
        return jnp.where(count(lambda k, _: k >= cand) >= topk, cand, thr)

    thr = lax.fori_loop(0, 32, bit_body, jnp.full((QBLK, 1), INT_MIN, I32))
    c_gt = count(lambda k, _: k > thr)
    c_eq = count(lambda k, _: k >= thr) - c_gt
    need = topk - c_gt

    cut_ref[...] = jnp.full(cut_ref.shape, 2 ** nbits, I32)

    @pl.when(jnp.max(jnp.where(c_eq > need, 1.0, 0.0)) > 0.0)
    def _():
        def tie_body(i, cut):
            cand = cut + lax.shift_left(jnp.int32(1), nbits - 1 - i)
            below = count(lambda k, kpos: (k == thr) & (kpos < cand))
            return jnp.where(below < need, cand, cut)
        cut = lax.fori_loop(0, nbits, tie_body, jnp.zeros((QBLK, 1), I32))
        cut_ref[...] = jnp.broadcast_to(cut, cut_ref.shape)

    cut = cut_ref[:, 0:1]

    def bias_body(c, _):
        k = keys_ref[c]
        kpos = c * ck + lane_pos
        chosen = (k > thr) | ((k == thr) & (kpos <= cut))
        bias_ref[c] = jnp.where(chosen & (kpos <= t), 0.0, NEG)
        return 0

    lax.fori_loop(0, nch, bias_body, 0)

    qf = q_ref[0].astype(F32)
    heads = []
    half = N_HEADS * HEAD_DIM // 2
    for hp in range(2):
        qs = _stack_heads(qf[:, hp * half:(hp + 1) * half], half // LANES, HEAD_DIM)
        o = _flash(qs, k_ref.at[0], v_ref.at[0], lambda c: bias_ref[c], nch, ck)
        heads += [o[r * QBLK:(r + 1) * QBLK] for r in range(N_HEADS // 2)]
    bz = bz_ref[0]
    o_ref[0] = (_pair_heads(heads) * (bz * _sigmoid(bz))).astype(o_ref.dtype)


def _dsa_mixer(u16, u32):
    b, s, _ = u16.shape
    ck = min(KCHUNK, s)
    w = N_HEADS * HEAD_DIM
    iqw = IDX_HEADS * IDX_DIM

    def kv_spec(name):
        return pl.BlockSpec((1, s, LANES), lambda bi, qi: (bi, 0, _OFF16[name] // LANES))

    return pl.pallas_call(
        functools.partial(_dsa_kernel, ck=ck, topk=min(DSA_TOPK, s // 4), nbits=int(s).bit_length()),
        grid=(b, s // QBLK),
        in_specs=[pl.BlockSpec((1, QBLK, w), lambda bi, qi: (bi, qi, _OFF16["b_q"] // w)),
                  pl.BlockSpec((1, QBLK, iqw), lambda bi, qi: (bi, qi, _OFF16["i_q"] // iqw)),
                  kv_spec("b_k"), kv_spec("b_v"), kv_spec("i_k"),
                  pl.BlockSpec((1, QBLK, LANES), lambda bi, qi: (bi, qi, _OFF32["i_w"] // LANES)),
                  pl.BlockSpec((1, QBLK, w), lambda bi, qi: (bi, qi, _OFF32["b_z"] // w))],
        out_specs=pl.BlockSpec((1, QBLK, w), lambda bi, qi: (bi, qi, 0)),
        out_shape=jax.ShapeDtypeStruct((b, s, w), _MXU_DTYPE),
        scratch_shapes=[pltpu.VMEM((s // ck, QBLK, ck), I32),
                        pltpu.VMEM((s // ck, QBLK, ck), F32),
                        pltpu.VMEM((QBLK, LANES), I32)],
        compiler_params=pltpu.CompilerParams(dimension_semantics=("arbitrary", "arbitrary"),
                                             vmem_limit_bytes=VMEM_LIMIT),
        name="dsa_mixer",
    )(u16, u16, u16, u16, u16, u32, u32)


def _merge_kernel(x_ref, ya_ref, yb_ref, yc_ref, g0_ref, g1_ref, g2_ref, wa_ref, wb_ref, wc_ref, wo_ref,
                  lg_ref, lb_ref, o32_ref, o16_ref, *, alpha):
    m = (_sigmoid(g0_ref[...]) * _dot(ya_ref[...], wa_ref[...])
         + _sigmoid(g1_ref[...]) * _dot(yb_ref[...], wb_ref[...])
         + _sigmoid(g2_ref[...]) * _dot(yc_ref[...], wc_ref[...]))
    z = alpha * x_ref[...] + _dot(m.astype(_MXU_DTYPE), wo_ref[...])
    mu = jnp.mean(z, axis=1, keepdims=True)
    zc = z - mu
    var = jnp.mean(zc * zc, axis=1, keepdims=True)
    y = zc * lax.rsqrt(var + LN_EPS) * lg_ref[...] + lb_ref[...]
    o32_ref[...] = y
    o16_ref[...] = y.astype(o16_ref.dtype)


def _merge(x, ya, yb, yc, u32, wa, wb, wc, wo, lg, lb, alpha, tm=512):
    m, d = x.shape
    w = ya.shape[1]
    gbase = _OFF32["g_merge"] // d
    row = lambda i: (i, 0)
    const = lambda i: (0, 0)
    return pl.pallas_call(
        functools.partial(_merge_kernel, alpha=alpha),
        grid=(m // tm,),
        in_specs=[pl.BlockSpec((tm, d), row),
                  pl.BlockSpec((tm, w), row), pl.BlockSpec((tm, w), row), pl.BlockSpec((tm, w), row),
                  pl.BlockSpec((tm, d), lambda i: (i, gbase)),
                  pl.BlockSpec((tm, d), lambda i: (i, gbase + 1)),
                  pl.BlockSpec((tm, d), lambda i: (i, gbase + 2)),
                  pl.BlockSpec((w, d), const), pl.BlockSpec((w, d), const), pl.BlockSpec((w, d), const),
                  pl.BlockSpec((d, d), const), pl.BlockSpec((1, d), const), pl.BlockSpec((1, d), const)],
        out_specs=[pl.BlockSpec((tm, d), row), pl.BlockSpec((tm, d), row)],
        out_shape=[jax.ShapeDtypeStruct((m, d), F32), jax.ShapeDtypeStruct((m, d), _MXU_DTYPE)],
        compiler_params=pltpu.CompilerParams(dimension_semantics=("arbitrary",), vmem_limit_bytes=VMEM_LIMIT),
        name="merge_out_ln",
    )(x, ya, yb, yc, u32, u32, u32, wa, wb, wc, wo, lg, lb)


def _overlap_matrix(ncp, s):
    n_c = (s - CMP_LEN) // CMP_STRIDE + 1
    c_start = np.arange(ncp) * CMP_STRIDE
    s_start = np.arange(s // SLC_LEN) * SLC_LEN
    ov = (c_start[:, None] <= s_start[None, :] + SLC_LEN - 1) & (c_start[:, None] + CMP_LEN - 1 >= s_start[None, :])
    ov &= (np.arange(ncp) < n_c)[:, None]
    return ov.astype(np.float32)


def _expand_matrix(s, ck):
    tok = np.arange(s).reshape(s // ck, 1, ck)
    return (tok // SLC_LEN == np.arange(s // SLC_LEN)[None, :, None]).astype(np.float32)


def _layer(x32, x16, p, consts, alpha):
    b, s, d = x32.shape
    m = b * s
    x16 = x16.reshape(m, d)
    u32 = _matmul_bias(x16, p["w32"], p["b32"], F32, min(1024, m), _TN32, "in_proj_f32").reshape(b, s, _N32)
    u16 = _matmul_bias(x16, p["w16"], p["b16"], _MXU_DTYPE, min(1024, m), _TN16, "in_proj_bf16").reshape(b, s, _N16)

    y_a = _pool_mixer(u32, p["pool_w"], p["pool_b"], p["pool_scale"])

    nch16 = s // CMP_STRIDE
    tok = u32[:, :, _OFF32["c_kc"]:_OFF32["c_kc"] + 2 * LANES]
    chunks = (tok.reshape(b, nch16, CMP_STRIDE, 2, C_KV_GROUPS, HEAD_DIM)
              .transpose(3, 0, 4, 1, 2, 5).reshape(2, b * C_KV_GROUPS, nch16, CMP_STRIDE * HEAD_DIM))
    kvcmp = _compress(chunks, p["cmp_pos"], p["cmp_w1"], p["cmp_w2"])

    y_c = _nsa_mixer(u16, u32, kvcmp, consts["overlap"], consts["expand"])
    y_b = _dsa_mixer(u16, u32)

    x32n, x16n = _merge(x32.reshape(m, d), y_a.reshape(m, -1), y_b.reshape(m, -1), y_c.reshape(m, -1),
                        u32.reshape(m, _N32), p["w_pa"], p["w_pb"], p["w_pc"], p["w_o"], p["ln_g"], p["ln_b"], alpha)
    return x32n.reshape(b, s, d), x16n.reshape(b, s, d)


def _prepare_params(w_in, b_in, pool_w, pool_b, pool_scale, cmp_pos_k, cmp_pos_v, cmp_w1_k, cmp_w2_k,
                    cmp_w1_v, cmp_w2_v, w_proj_a, w_proj_b, w_proj_c, w_o, ln_g, ln_b):
    nl = w_in.shape[0]
    mx = _MXU_DTYPE
    w_ext = jnp.concatenate([w_in, jnp.zeros((nl, w_in.shape[1], 1), w_in.dtype)], axis=2)
    b_ext = jnp.concatenate([b_in, jnp.zeros((nl, 1), b_in.dtype)], axis=1)
    half = CMP_LEN // 2
    pos = jnp.stack([cmp_pos_k, cmp_pos_v], axis=1).reshape(nl, 2, 2, half * HEAD_DIM)
    return {
        "w32": (jnp.take(w_ext, _IDX32, axis=2) * _SCALE32).astype(mx),
        "b32": (jnp.take(b_ext, _IDX32, axis=1) * _SCALE32)[:, None, :],
        "w16": (jnp.take(w_ext, _IDX16, axis=2) * _SCALE16).astype(mx),
        "b16": (jnp.take(b_ext, _IDX16, axis=1) * _SCALE16)[:, None, :],
        "pool_w": pool_w.astype(mx),
        "pool_b": pool_b.reshape(nl, 1, -1),
        "pool_scale": pool_scale.reshape(nl, 1, -1),
        "cmp_pos": pos,
        "cmp_w1": jnp.stack([cmp_w1_k, cmp_w1_v], axis=1).astype(mx),
        "cmp_w2": jnp.concatenate([jnp.stack([cmp_w2_k, cmp_w2_v], axis=1)] * 2, axis=-1).astype(mx),
        "w_pa": w_proj_a.astype(mx), "w_pb": w_proj_b.astype(mx), "w_pc": w_proj_c.astype(mx),
        "w_o": w_o.astype(mx),
        "ln_g": ln_g[:, None, :], "ln_b": ln_b[:, None, :],
    }


def kernel(x, w_in, b_in, pool_w, pool_b, pool_scale, cmp_pos_k, cmp_pos_v, cmp_w1_k, cmp_w2_k, cmp_w1_v, cmp_w2_v, w_proj_a, w_proj_b, w_proj_c, w_o, ln_g, ln_b):
    depth = w_in.shape[0]
    s = x.shape[1]
    params = _prepare_params(w_in, b_in, pool_w, pool_b, pool_scale, cmp_pos_k, cmp_pos_v, cmp_w1_k, cmp_w2_k,
                             cmp_w1_v, cmp_w2_v, w_proj_a, w_proj_b, w_proj_c, w_o, ln_g, ln_b)
    ck = min(KCHUNK, s)
    consts = {"overlap": jnp.asarray(_overlap_matrix(s // CMP_STRIDE, s), _MXU_DTYPE),
              "expand": jnp.asarray(_expand_matrix(s, ck), _MXU_DTYPE)}
    alpha = (2 * depth) ** 0.25
    h32, h16 = x, x.astype(_MXU_DTYPE)
    for l in range(depth):
        h32, h16 = _layer(h32, h16, {k: v[l] for k, v in params.items()}, consts, alpha)
    return h32
```

```python
import functools

import numpy as np
import jax
import jax.numpy as jnp
from jax import lax
from jax.experimental import pallas as pl
from jax.experimental.pallas import tpu as pltpu

F32 = jnp.float32
I32 = jnp.int32
I16 = jnp.int16
_MXU_DTYPE = jnp.bfloat16

D_MODEL = 1024
HEAD_DIM = 64
LANES = 128
POOL_WINDOWS = (2, 4, 8, 16)
POOL_GC = 128
N_HEADS = 8
IDX_HEADS = 8
IDX_DIM = 32
DSA_TOPK = 256
C_KV_GROUPS = 2
HEADS_PER_GROUP = N_HEADS // C_KV_GROUPS
CMP_LEN = 32
CMP_STRIDE = 16
SLC_LEN = 64
SLC_N = 16
WIN = 512
FORCE_BONUS = 1e4
LN_EPS = 1e-5
NEG = -1e30
QBLK = 128
KCHUNK = 512
HALF16 = 2 ** 15
VMEM_LIMIT = 56 * 1024 * 1024

_IN_WIDTHS = (512, 512, 512, 64, 64, 512, 256, 32, 8, 512, 128, 128, 128, 128, 128, 128, 24, 512, 3072)
_IN_NAMES = ("a_x", "a_z", "b_q", "b_k", "b_v", "b_z", "i_q", "i_k", "i_w", "c_q", "c_kc", "c_vc",
             "c_ks", "c_vs", "c_kw", "c_vw", "c_g", "c_z", "g_merge")
_N_IN = sum(_IN_WIDTHS)
_OFF = dict(zip(_IN_NAMES, np.cumsum((0,) + _IN_WIDTHS[:-1])))
_WID = dict(zip(_IN_NAMES, _IN_WIDTHS))


def _seg(name, lo=0, hi=None):
    hi = _WID[name] if hi is None else hi
    return np.arange(_OFF[name] + lo, _OFF[name] + hi)


def _pad(n):
    return np.full((n,), _N_IN)


def _layout32():
    segs, off, pos = [], {}, 0

    def add(name, idx):
        nonlocal pos
        off[name] = pos
        segs.append(idx)
        pos += len(idx)

    add("a_x", _seg("a_x"))
    add("a_z", _seg("a_z"))
    add("b_z", _seg("b_z"))
    add("c_z", _seg("c_z"))
    add("g_merge", _seg("g_merge"))
    add("c_kc", _seg("c_kc"))
    add("c_vc", _seg("c_vc"))
    add("i_w", np.concatenate([_seg("i_w"), _pad(LANES - 8)]))
    for g in range(C_KV_GROUPS):
        add(f"c_g{g}", np.concatenate([_seg("c_g", 12 * g, 12 * g + 12), _pad(LANES - 12)]))
    idx = np.concatenate(segs)
    return idx, np.ones((len(idx),), np.float32), off


def _layout16():
    segs, scales, off, pos = [], [], {}, 0

    def add(name, idx, scale=1.0):
        nonlocal pos
        off[name] = pos
        segs.append(idx)
        scales.append(np.full((len(idx),), scale, np.float32))
        pos += len(idx)

    qk_scale = HEAD_DIM ** -0.5 * float(np.log2(np.e))
    add("b_q", _seg("b_q"), qk_scale)
    add("c_q", _seg("c_q"), qk_scale)
    add("i_q", _seg("i_q"))
    for name in ("c_ks", "c_vs", "c_kw", "c_vw"):
        for g in range(C_KV_GROUPS):
            one = _seg(name, HEAD_DIM * g, HEAD_DIM * (g + 1))
            add(f"{name}{g}", np.concatenate([one, one]))
    add("b_k", np.concatenate([_seg("b_k")] * 2))
    add("b_v", np.concatenate([_seg("b_v")] * 2))
    add("i_k", np.concatenate([_seg("i_k")] * (LANES // IDX_DIM)))
    return np.concatenate(segs), np.concatenate(scales), off


def _gather_cols(a, idx):
    pieces, i = [], 0
    while i < len(idx):
        j = i + 1
        if idx[i] == _N_IN:
            while j < len(idx) and idx[j] == _N_IN:
                j += 1
            pieces.append(jnp.zeros(a.shape[:-1] + (j - i,), a.dtype))
        else:
            while j < len(idx) and idx[j] == idx[j - 1] + 1:
                j += 1
            pieces.append(a[..., int(idx[i]):int(idx[i]) + (j - i)])
        i = j
    return jnp.concatenate(pieces, axis=-1)


_IDX32, _SCALE32, _OFF32 = _layout32()
_IDX16, _SCALE16, _OFF16 = _layout16()
_N32 = len(_IDX32)
_N16 = len(_IDX16)
_TN32 = 640
_TN16 = 384
assert _N32 % _TN32 == 0 and _N16 % _TN16 == 0


def _sigmoid(x):
    return 1.0 / (1.0 + jnp.exp(-x))


def _dot(a, b):
    return jnp.dot(a, b, preferred_element_type=F32)


def _dot_nt(a, b):
    return lax.dot_general(a, b, (((1,), (1,)), ((), ())), preferred_element_type=F32)


def _mm_bias_kernel(x_ref, w_ref, b_ref, o_ref):
    o_ref[...] = (_dot(x_ref[...], w_ref[...]) + b_ref[...]).astype(o_ref.dtype)


def _matmul_bias(x, w, b, out_dtype, tm, tn, name):
    m, k = x.shape
    n = w.shape[1]
    return pl.pallas_call(
        _mm_bias_kernel,
        grid=(m // tm, n // tn),
        in_specs=[pl.BlockSpec((tm, k), lambda i, j: (i, 0)),
                  pl.BlockSpec((k, tn), lambda i, j: (0, j)),
                  pl.BlockSpec((1, tn), lambda i, j: (0, j))],
        out_specs=pl.BlockSpec((tm, tn), lambda i, j: (i, j)),
        out_shape=jax.ShapeDtypeStruct((m, n), out_dtype),
        compiler_params=pltpu.CompilerParams(dimension_semantics=("arbitrary", "arbitrary"),
                                             vmem_limit_bytes=VMEM_LIMIT),
        name=name,
    )(x, w, b)


_HALO = 16


def _pool_kernel(xa_ref, halo_ref, az_ref, pw_ref, pb_ref, ps_ref, o_ref, *, tb):
    i = pl.program_id(1)
    cur = xa_ref[0]
    halo = jnp.where(i > 0, halo_ref[0], 0.0)
    ext = jnp.concatenate([halo, cur], axis=0)
    pos = (i * tb + 1 + lax.broadcasted_iota(I32, (tb, 1), 0)).astype(F32)
    outs = []
    for g, wnd in enumerate(POOL_WINDOWS):
        s = ext[:, g * POOL_GC:(g + 1) * POOL_GC]
        k = 1
        while k < wnd:
            s = s + pltpu.roll(s, k, axis=0)
            k *= 2
        mean = s[_HALO:] / jnp.minimum(pos, float(wnd))
        pooled = mean - cur[:, g * POOL_GC:(g + 1) * POOL_GC]
        outs.append(_dot(pooled.astype(_MXU_DTYPE), pw_ref[g]))
    y = jnp.concatenate(outs, axis=1) + pb_ref[...]
    az = az_ref[0]
    o_ref[0] = (y * ps_ref[...] * (az * _sigmoid(az))).astype(o_ref.dtype)


def _pool_mixer(u32, pool_w, pool_b, pool_scale, tb=512):
    b, s, _ = u32.shape
    hb = tb // _HALO
    return pl.pallas_call(
        functools.partial(_pool_kernel, tb=tb),
        grid=(b, s // tb),
        in_specs=[pl.BlockSpec((1, tb, 512), lambda bi, i: (bi, i, _OFF32["a_x"] // 512)),
                  pl.BlockSpec((1, _HALO, 512), lambda bi, i: (bi, jnp.maximum(i * hb - 1, 0), _OFF32["a_x"] // 512)),
                  pl.BlockSpec((1, tb, 512), lambda bi, i: (bi, i, _OFF32["a_z"] // 512)),
                  pl.BlockSpec((4, POOL_GC, POOL_GC), lambda bi, i: (0, 0, 0)),
                  pl.BlockSpec((1, 512), lambda bi, i: (0, 0)),
                  pl.BlockSpec((1, 512), lambda bi, i: (0, 0))],
        out_specs=pl.BlockSpec((1, tb, 512), lambda bi, i: (bi, i, 0)),
        out_shape=jax.ShapeDtypeStruct((b, s, 512), _MXU_DTYPE),
        compiler_params=pltpu.CompilerParams(dimension_semantics=("arbitrary", "arbitrary"),
                                             vmem_limit_bytes=VMEM_LIMIT),
        name="pool_mixer",
    )(u32, u32, u32, pool_w, pool_b, pool_scale)


def _compress_kernel(ch_ref, pos_ref, w1_ref, w2_ref, o_ref):
    ch = ch_ref[0, 0]
    pos = pos_ref[0]
    w1 = w1_ref[0]
    half = ch.shape[1]
    n = ch.shape[0]
    a = _dot((ch + pos[0:1]).astype(_MXU_DTYPE), w1[:half])
    bb = _dot((ch + pos[1:2]).astype(_MXU_DTYPE), w1[half:])
    h = a + pltpu.roll(bb, n - 1, axis=0)
    h = h * _sigmoid(h)
    o_ref[0, 0] = _dot(h.astype(_MXU_DTYPE), w2_ref[0]).astype(o_ref.dtype)


def _compress(chunks, pos, w1, w2dup):
    _, bg, n, width = chunks.shape
    return pl.pallas_call(
        _compress_kernel,
        grid=(2, bg),
        in_specs=[pl.BlockSpec((1, 1, n, width), lambda kv, i: (kv, i, 0, 0)),
                  pl.BlockSpec((1, 2, width), lambda kv, i: (kv, 0, 0)),
                  pl.BlockSpec((1, 2 * width, HEAD_DIM), lambda kv, i: (kv, 0, 0)),
                  pl.BlockSpec((1, HEAD_DIM, LANES), lambda kv, i: (kv, 0, 0))],
        out_specs=pl.BlockSpec((1, 1, n, LANES), lambda kv, i: (kv, i, 0, 0)),
        out_shape=jax.ShapeDtypeStruct((2, bg, n, LANES), _MXU_DTYPE),
        compiler_params=pltpu.CompilerParams(dimension_semantics=("arbitrary", "arbitrary"),
                                             vmem_limit_bytes=VMEM_LIMIT),
        name="nsa_compress",
    )(chunks, pos, w1, w2dup)


def _stack_heads(qf, n_tiles, width):
    lane = lax.broadcasted_iota(I32, (1, LANES), 1)
    per_tile = LANES // width
    parts = []
    for tix in range(n_tiles):
        qt = qf[:, tix * LANES:(tix + 1) * LANES]
        for j in range(per_tile):
            keep = (lane >= j * width) & (lane < (j + 1) * width)
            parts.append(jnp.where(keep, qt, 0.0))
    return jnp.concatenate(parts, axis=0).astype(_MXU_DTYPE)


def _flash(qs_list, k_ref, v_ref, bias_fn, nchunks, ck):
    rows = qs_list[0].shape[0]
    reps = rows // QBLK

    def body(c, carry):
        off = pl.multiple_of(c * ck, ck)
        kc = k_ref[pl.ds(off, ck), :]
        vc = v_ref[pl.ds(off, ck), :]
        bias = jnp.concatenate([bias_fn(c)] * reps, axis=0)
        out = []
        for qs, (m, l, acc) in zip(qs_list, carry):
            s = _dot_nt(qs, kc) + bias
            m_new = jnp.maximum(m, jnp.max(s, axis=1, keepdims=True))
            a = jnp.exp2(m - m_new)
            p = jnp.exp2(s - m_new)
            l = a * l + jnp.sum(p, axis=1, keepdims=True)
            acc = a * acc + _dot(p.astype(_MXU_DTYPE), vc)
            out.append((m_new, l, acc))
        return tuple(out)

    init = tuple((jnp.full((rows, 1), -1e38, F32), jnp.zeros((rows, 1), F32), jnp.zeros((rows, LANES), F32))
                 for _ in qs_list)
    final = lax.fori_loop(0, nchunks, body, init)
    return [acc / l for _, l, acc in final]


def _masked_softmax_pv(s, mask, v):
    s = jnp.where(mask, s, NEG)
    m = jnp.max(s, axis=1, keepdims=True)
    p = jnp.exp2(s - m)
    p = jnp.where(mask, p / jnp.sum(p, axis=1, keepdims=True), 0.0).astype(_MXU_DTYPE)
    return p, _dot(p, v)


def _pair_heads(heads):
    lo = lax.broadcasted_iota(I32, (1, LANES), 1) < HEAD_DIM
    tiles = [jnp.where(lo, heads[i], heads[i + 1]) for i in range(0, len(heads), 2)]
    return jnp.concatenate(tiles, axis=1)


def _nsa_kernel(q_ref, kc_ref, vc_ref, ks_ref, vs_ref, kw_ref, vw_ref, cg_ref, cz_ref, ov_ref, e_ref, o_ref,
                *, ck, win, n_sel):
    qi = pl.program_id(2)
    q0 = qi * QBLK
    hpg = HEADS_PER_GROUP
    t = q0 + lax.broadcasted_iota(I32, (QBLK, 1), 0)
    t4 = jnp.concatenate([t] * hpg, axis=0)
    qs = _stack_heads(q_ref[0].astype(F32), hpg // 2, HEAD_DIM)

    ncp = kc_ref.shape[2]
    cend = lax.broadcasted_iota(I32, (1, ncp), 1) * CMP_STRIDE + (CMP_LEN - 1)
    p_c, o_c = _masked_softmax_pv(_dot_nt(qs, kc_ref[0, 0]), cend <= t4, vc_ref[0, 0])

    imp4 = _dot(p_c, ov_ref[...])
    imp = imp4[0:QBLK]
    for r in range(1, hpg):
        imp = imp + imp4[r * QBLK:(r + 1) * QBLK]
    n_s = imp.shape[1]
    work = imp.T
    ji = lax.broadcasted_iota(I32, (n_s, 1), 0)
    jf = ji.astype(F32)
    tq = q0 + lax.broadcasted_iota(I32, (1, QBLK), 1)
    blk = lax.shift_right_logical(tq, 6)
    forced = (ji == 0) | (ji == blk) | (ji == blk - 1)
    work = work + jnp.where(forced, FORCE_BONUS, 0.0)
    work = jnp.where(ji * SLC_LEN <= tq, work, NEG)
    sel = jnp.zeros((n_s, QBLK), F32)
    for _ in range(n_sel):
        mx = jnp.max(work, axis=0, keepdims=True)
        first = jnp.min(jnp.where(work == mx, jf, float(n_s)), axis=0, keepdims=True)
        hit = jf == first
        sel = jnp.where(hit, 1.0, sel)
        work = jnp.where(hit, -3e38, work)
    selb = jnp.where(sel.T > 0.5, 0.0, NEG).astype(_MXU_DTYPE)

    def bias_fn(c):
        kpos = c * ck + lax.broadcasted_iota(I32, (1, ck), 1)
        return jnp.where(kpos <= t, _dot(selb, e_ref[c]), NEG)

    o_s, = _flash([qs], ks_ref.at[0], vs_ref.at[0], bias_fn, q0 // ck + 1, ck)

    span = win + QBLK
    start = pl.multiple_of(jnp.maximum(q0 - win, 0), QBLK)
    kpos = start + lax.broadcasted_iota(I32, (1, span), 1)
    mask_w = (kpos <= t4) & (kpos > t4 - win)
    _, o_w = _masked_softmax_pv(_dot_nt(qs, kw_ref[0, pl.ds(start, span), :]), mask_w,
                                vw_ref[0, pl.ds(start, span), :])

    gb = _sigmoid(cg_ref[0][:, 0:3 * hpg])
    heads = []
    for r in range(hpg):
        rows = slice(r * QBLK, (r + 1) * QBLK)
        heads.append(gb[:, 3 * r:3 * r + 1] * o_c[rows] + gb[:, 3 * r + 1:3 * r + 2] * o_s[rows]
                     + gb[:, 3 * r + 2:3 * r + 3] * o_w[rows])
    cz = cz_ref[0]
    o_ref[0] = (_pair_heads(heads) * (cz * _sigmoid(cz))).astype(o_ref.dtype)


def _nsa_mixer(u16, u32, kvcmp, overlap, expand):
    b, s, _ = u16.shape
    ck = min(KCHUNK, s)
    gw = HEADS_PER_GROUP * HEAD_DIM
    ncp = kvcmp.shape[2]

    def kv_spec(name):
        base = _OFF16[name + "0"] // LANES
        return pl.BlockSpec((1, s, LANES), lambda bi, g, qi: (bi, 0, base + g))

    return pl.pallas_call(
        functools.partial(_nsa_kernel, ck=ck, win=WIN, n_sel=min(SLC_N, s // SLC_LEN)),
        grid=(b, C_KV_GROUPS, s // QBLK),
        in_specs=[pl.BlockSpec((1, QBLK, gw), lambda bi, g, qi: (bi, qi, _OFF16["c_q"] // gw + g)),
                  pl.BlockSpec((1, 1, ncp, LANES), lambda bi, g, qi: (0, bi * C_KV_GROUPS + g, 0, 0)),
                  pl.BlockSpec((1, 1, ncp, LANES), lambda bi, g, qi: (1, bi * C_KV_GROUPS + g, 0, 0)),
                  kv_spec("c_ks"), kv_spec("c_vs"), kv_spec("c_kw"), kv_spec("c_vw"),
                  pl.BlockSpec((1, QBLK, LANES), lambda bi, g, qi: (bi, qi, _OFF32["c_g0"] // LANES + g)),
                  pl.BlockSpec((1, QBLK, gw), lambda bi, g, qi: (bi, qi, _OFF32["c_z"] // gw + g)),
                  pl.BlockSpec(overlap.shape, lambda bi, g, qi: (0, 0)),
                  pl.BlockSpec(expand.shape, lambda bi, g, qi: (0, 0, 0))],
        out_specs=pl.BlockSpec((1, QBLK, gw), lambda bi, g, qi: (bi, qi, g)),
        out_shape=jax.ShapeDtypeStruct((b, s, N_HEADS * HEAD_DIM), _MXU_DTYPE),
        compiler_params=pltpu.CompilerParams(dimension_semantics=("arbitrary", "arbitrary", "arbitrary"),
                                             vmem_limit_bytes=VMEM_LIMIT),
        name="nsa_mixer",
    )(u16, kvcmp, kvcmp, u16, u16, u16, u16, u32, u32, overlap, expand)


def _dsa_kernel(q_ref, iq_ref, k_ref, v_ref, ik_ref, iw_ref, bz_ref, o_ref, keys_ref, hi_ref, lo_ref, bias_ref,
                cut_ref, *, ck, topk, nbits):
    qi = pl.program_id(1)
    q0 = qi * QBLK
    nch = q0 // ck + 1
    t = q0 + lax.broadcasted_iota(I32, (QBLK, 1), 0)
    lane_pos = lax.broadcasted_iota(I32, (1, ck), 1)
    fold = ck // LANES

    iqs = _stack_heads(iq_ref[0].astype(F32), IDX_HEADS * IDX_DIM // LANES, IDX_DIM)
    iw = iw_ref[0] * (IDX_HEADS ** -0.5 * IDX_DIM ** -0.5)
    wcols = [jnp.broadcast_to(iw[:, h:h + 1], (QBLK, ck)) for h in range(IDX_HEADS)]

    def score_body(c, _):
        off = pl.multiple_of(c * ck, ck)
        rel = _dot_nt(iqs, ik_ref[0, pl.ds(off, ck), :])
        sc = jnp.maximum(rel[0:QBLK], 0.0) * wcols[0]
        for h in range(1, IDX_HEADS):
            sc = sc + jnp.maximum(rel[h * QBLK:(h + 1) * QBLK], 0.0) * wcols[h]
        sc = jnp.where(c * ck + lane_pos <= t, sc, NEG)
        sc = jnp.where(sc == 0.0, 0.0, sc)
        bits = pltpu.bitcast(sc, I32)
        key = bits ^ (lax.shift_right_arithmetic(bits, 31) & 0x7FFFFFFF)
        keys_ref[c] = key
        hi_ref[c] = lax.shift_right_arithmetic(key, 16).astype(I16)
        lo_ref[c] = ((key & 0xFFFF) - HALF16).astype(I16)
        return 0

    lax.fori_loop(0, nch, score_body, 0)

    def count(pred):
        def body(c, acc):
            ind = jnp.where(pred(keys_ref[c], c * ck + lane_pos), 1.0, 0.0)
            part = ind[:, 0:LANES]
            for i in range(1, fold):
                part = part + ind[:, i * LANES:(i + 1) * LANES]
            return acc + part
        acc = lax.fori_loop(0, nch, body, jnp.zeros((QBLK, LANES), F32))
        return jnp.sum(acc, axis=1, keepdims=True)

    one16 = jnp.ones((QBLK, LANES), I16)
    zero16 = jnp.zeros((QBLK, LANES), I16)

    def count16(ref, cand):
        cand16 = cand.astype(I16)

        def body(c, acc):
            x = ref[c]
            for i in range(fold):
                acc = acc + jnp.where(x[:, i * LANES:(i + 1) * LANES] >= cand16, one16, zero16)
            return acc

        acc = lax.fori_loop(0, nch, body, zero16)
        tot = jnp.sum(acc.astype(F32), axis=1, keepdims=True)
        return jnp.broadcast_to(tot, (QBLK, LANES))

    def search16(ref, want):
        def bit_body(i, thr):
            cand = thr + lax.shift_left(jnp.int32(1), 15 - i)
            return jnp.where(count16(ref, cand) >= want, cand, thr)
        return lax.fori_loop(0, 16, bit_body, jnp.full((QBLK, LANES), -HALF16, I32))

    thr_hi = search16(hi_ref, float(topk))
    above = count16(hi_ref, thr_hi + 1)
    thr_hi16 = thr_hi.astype(I16)
    min16 = jnp.full((QBLK, LANES), -HALF16, I16)

    def bucket_body(c, _):
        hi = hi_ref[c]
        lo = lo_ref[c]
        lo_ref[c] = jnp.concatenate(
            [jnp.where(hi[:, i * LANES:(i + 1) * LANES] == thr_hi16, lo[:, i * LANES:(i + 1) * LANES], min16)
             for i in range(fold)], axis=1)
        return 0

    lax.fori_loop(0, nch, bucket_body, 0)
    thr_lo = search16(lo_ref, topk - above)
    thr = thr_hi[:, 0:1] * 65536 + (thr_lo[:, 0:1] + HALF16)
    c_gt = count(lambda k, _: k > thr)
    c_eq = count(lambda k, _: k >= thr) - c_gt
    need = topk - c_gt

    cut_ref[...] = jnp.full(cut_ref.shape, 2 ** nbits, I32)

    @pl.when(jnp.max(jnp.where(c_eq > need, 1.0, 0.0)) > 0.0)
    def _():
        def tie_body(i, cut):
            cand = cut + lax.shift_left(jnp.int32(1), nbits - 1 - i)
            below = count(lambda k, kpos: (k == thr) & (kpos < cand))
            return jnp.where(below < need, cand, cut)
        cut = lax.fori_loop(0, nbits, tie_body, jnp.zeros((QBLK, 1), I32))
        cut_ref[...] = jnp.broadcast_to(cut, cut_ref.shape)

    cut = cut_ref[:, 0:1]

    def bias_body(c, _):
        k = keys_ref[c]
        kpos = c * ck + lane_pos
        chosen = (k > thr) | ((k == thr) & (kpos <= cut))
        bias_ref[c] = jnp.where(chosen & (kpos <= t), 0.0, NEG)
        return 0

    lax.fori_loop(0, nch, bias_body, 0)

    qf = q_ref[0].astype(F32)
    half = N_HEADS * HEAD_DIM // 2
    qs_list = [_stack_heads(qf[:, hp * half:(hp + 1) * half], half // LANES, HEAD_DIM) for hp in range(2)]
    outs = _flash(qs_list, k_ref.at[0], v_ref.at[0], lambda c: bias_ref[c], nch, ck)
    heads = [o[r * QBLK:(r + 1) * QBLK] for o in outs for r in range(N_HEADS // 2)]
    bz = bz_ref[0]
    o_ref[0] = (_pair_heads(heads) * (bz * _sigmoid(bz))).astype(o_ref.dtype)


def _dsa_mixer(u16, u32):
    b, s, _ = u16.shape
    ck = min(KCHUNK, s)
    w = N_HEADS * HEAD_DIM
    iqw = IDX_HEADS * IDX_DIM

    def kv_spec(name):
        return pl.BlockSpec((1, s, LANES), lambda bi, qi: (bi, 0, _OFF16[name] // LANES))

    return pl.pallas_call(
        functools.partial(_dsa_kernel, ck=ck, topk=min(DSA_TOPK, s // 4), nbits=int(s).bit_length()),
        grid=(b, s // QBLK),
        in_specs=[pl.BlockSpec((1, QBLK, w), lambda bi, qi: (bi, qi, _OFF16["b_q"] // w)),
                  pl.BlockSpec((1, QBLK, iqw), lambda bi, qi: (bi, qi, _OFF16["i_q"] // iqw)),
                  kv_spec("b_k"), kv_spec("b_v"), kv_spec("i_k"),
                  pl.BlockSpec((1, QBLK, LANES), lambda bi, qi: (bi, qi, _OFF32["i_w"] // LANES)),
                  pl.BlockSpec((1, QBLK, w), lambda bi, qi: (bi, qi, _OFF32["b_z"] // w))],
        out_specs=pl.BlockSpec((1, QBLK, w), lambda bi, qi: (bi, qi, 0)),
        out_shape=jax.ShapeDtypeStruct((b, s, w), _MXU_DTYPE),
        scratch_shapes=[pltpu.VMEM((s // ck, QBLK, ck), I32),
                        pltpu.VMEM((s // ck, QBLK, ck), I16),
                        pltpu.VMEM((s // ck, QBLK, ck), I16),
                        pltpu.VMEM((s // ck, QBLK, ck), F32),
                        pltpu.VMEM((QBLK, LANES), I32)],
        compiler_params=pltpu.CompilerParams(dimension_semantics=("arbitrary", "arbitrary"),
                                             vmem_limit_bytes=VMEM_LIMIT),
        name="dsa_mixer",
    )(u16, u16, u16, u16, u16, u32, u32)


def _merge_kernel(x_ref, ya_ref, yb_ref, yc_ref, g0_ref, g1_ref, g2_ref, wa_ref, wb_ref, wc_ref, wo_ref,
                  lg_ref, lb_ref, o32_ref, o16_ref, *, alpha):
    m = (_sigmoid(g0_ref[...]) * _dot(ya_ref[...], wa_ref[...])
         + _sigmoid(g1_ref[...]) * _dot(yb_ref[...], wb_ref[...])
         + _sigmoid(g2_ref[...]) * _dot(yc_ref[...], wc_ref[...]))
    z = alpha * x_ref[...] + _dot(m.astype(_MXU_DTYPE), wo_ref[...])
    mu = jnp.mean(z, axis=1, keepdims=True)
    zc = z - mu
    var = jnp.mean(zc * zc, axis=1, keepdims=True)
    y = zc * lax.rsqrt(var + LN_EPS) * lg_ref[...] + lb_ref[...]
    o32_ref[...] = y
    o16_ref[...] = y.astype(o16_ref.dtype)


def _merge(x, ya, yb, yc, u32, wa, wb, wc, wo, lg, lb, alpha, tm=512):
    m, d = x.shape
    w = ya.shape[1]
    gbase = _OFF32["g_merge"] // d
    row = lambda i: (i, 0)
    const = lambda i: (0, 0)
    return pl.pallas_call(
        functools.partial(_merge_kernel, alpha=alpha),
        grid=(m // tm,),
        in_specs=[pl.BlockSpec((tm, d), row),
                  pl.BlockSpec((tm, w), row), pl.BlockSpec((tm, w), row), pl.BlockSpec((tm, w), row),
                  pl.BlockSpec((tm, d), lambda i: (i, gbase)),
                  pl.BlockSpec((tm, d), lambda i: (i, gbase + 1)),
                  pl.BlockSpec((tm, d), lambda i: (i, gbase + 2)),
                  pl.BlockSpec((w, d), const), pl.BlockSpec((w, d), const), pl.BlockSpec((w, d), const),
                  pl.BlockSpec((d, d), const), pl.BlockSpec((1, d), const), pl.BlockSpec((1, d), const)],
        out_specs=[pl.BlockSpec((tm, d), row), pl.BlockSpec((tm, d), row)],
        out_shape=[jax.ShapeDtypeStruct((m, d), F32), jax.ShapeDtypeStruct((m, d), _MXU_DTYPE)],
        compiler_params=pltpu.CompilerParams(dimension_semantics=("arbitrary",), vmem_limit_bytes=VMEM_LIMIT),
        name="merge_out_ln",
    )(x, ya, yb, yc, u32, u32, u32, wa, wb, wc, wo, lg, lb)


def _overlap_matrix(ncp, s):
    n_c = (s - CMP_LEN) // CMP_STRIDE + 1
    c_start = np.arange(ncp) * CMP_STRIDE
    s_start = np.arange(s // SLC_LEN) * SLC_LEN
    ov = (c_start[:, None] <= s_start[None, :] + SLC_LEN - 1) & (c_start[:, None] + CMP_LEN - 1 >= s_start[None, :])
    ov &= (np.arange(ncp) < n_c)[:, None]
    return ov.astype(np.float32)


def _expand_matrix(s, ck):
    tok = np.arange(s).reshape(s // ck, 1, ck)
    return (tok // SLC_LEN == np.arange(s // SLC_LEN)[None, :, None]).astype(np.float32)


def _layer(x32, x16, p, consts, alpha):
    b, s, d = x32.shape
    m = b * s
    x16 = x16.reshape(m, d)
    u32 = _matmul_bias(x16, p["w32"], p["b32"], F32, min(1024, m), _TN32, "in_proj_f32").reshape(b, s, _N32)
    u16 = _matmul_bias(x16, p["w16"], p["b16"], _MXU_DTYPE, min(1024, m), _TN16, "in_proj_bf16").reshape(b, s, _N16)

    y_a = _pool_mixer(u32, p["pool_w"], p["pool_b"], p["pool_scale"])

    nch16 = s // CMP_STRIDE
    tok = u32[:, :, _OFF32["c_kc"]:_OFF32["c_kc"] + 2 * LANES]
    chunks = (tok.reshape(b, nch16, CMP_STRIDE, 2, C_KV_GROUPS, HEAD_DIM)
              .transpose(3, 0, 4, 1, 2, 5).reshape(2, b * C_KV_GROUPS, nch16, CMP_STRIDE * HEAD_DIM))
    kvcmp = _compress(chunks, p["cmp_pos"], p["cmp_w1"], p["cmp_w2"])

    y_c = _nsa_mixer(u16, u32, kvcmp, consts["overlap"], consts["expand"])
    y_b = _dsa_mixer(u16, u32)

    x32n, x16n = _merge(x32.reshape(m, d), y_a.reshape(m, -1), y_b.reshape(m, -1), y_c.reshape(m, -1),
                        u32.reshape(m, _N32), p["w_pa"], p["w_pb"], p["w_pc"], p["w_o"], p["ln_g"], p["ln_b"], alpha)
    return x32n.reshape(b, s, d), x16n.reshape(b, s, d)


def _prepare_params(w_in, b_in, pool_w, pool_b, pool_scale, cmp_pos_k, cmp_pos_v, cmp_w1_k, cmp_w2_k,
                    cmp_w1_v, cmp_w2_v, w_proj_a, w_proj_b, w_proj_c, w_o, ln_g, ln_b):
    nl = w_in.shape[0]
    mx = _MXU_DTYPE
    half = CMP_LEN // 2
    pos = jnp.stack([cmp_pos_k, cmp_pos_v], axis=1).reshape(nl, 2, 2, half * HEAD_DIM)
    return {
        "w32": (_gather_cols(w_in, _IDX32) * _SCALE32).astype(mx),
        "b32": (_gather_cols(b_in, _IDX32) * _SCALE32)[:, None, :],
        "w16": (_gather_cols(w_in, _IDX16) * _SCALE16).astype(mx),
        "b16": (_gather_cols(b_in, _IDX16) * _SCALE16)[:, None, :],
        "pool_w": pool_w.astype(mx),
        "pool_b": pool_b.reshape(nl, 1, -1),
        "pool_scale": pool_scale.reshape(nl, 1, -1),
        "cmp_pos": pos,
        "cmp_w1": jnp.stack([cmp_w1_k, cmp_w1_v], axis=1).astype(mx),
        "cmp_w2": jnp.concatenate([jnp.stack([cmp_w2_k, cmp_w2_v], axis=1)] * 2, axis=-1).astype(mx),
        "w_pa": w_proj_a.astype(mx), "w_pb": w_proj_b.astype(mx), "w_pc": w_proj_c.astype(mx),
        "w_o": w_o.astype(mx),
        "ln_g": ln_g[:, None, :], "ln_b": ln_b[:, None, :],
    }


def kernel(x, w_in, b_in, pool_w, pool_b, pool_scale, cmp_pos_k, cmp_pos_v, cmp_w1_k, cmp_w2_k, cmp_w1_v, cmp_w2_v, w_proj_a, w_proj_b, w_proj_c, w_o, ln_g, ln_b):
    depth = w_in.shape[0]
    s = x.shape[1]
    params = _prepare_params(w_in, b_in, pool_w, pool_b, pool_scale, cmp_pos_k, cmp_pos_v, cmp_w1_k, cmp_w2_k,
                             cmp_w1_v, cmp_w2_v, w_proj_a, w_proj_b, w_proj_c, w_o, ln_g, ln_b)
    ck = min(KCHUNK, s)
    consts = {"overlap": jnp.asarray(_overlap_matrix(s // CMP_STRIDE, s), _MXU_DTYPE),
              "expand": jnp.asarray(_expand_matrix(s, ck), _MXU_DTYPE)}
    alpha = (2 * depth) ** 0.25
    h32, h16 = x, x.astype(_MXU_DTYPE)
    for l in range(depth):
        h32, h16 = _layer(h32, h16, {k: v[l] for k, v in params.items()}, consts, alpha)
    return h32
```

```python
import functools

import numpy as np
import jax
import jax.numpy as jnp
from jax import lax
from jax.experimental import pallas as pl
from jax.experimental.pallas import tpu as pltpu

F32 = jnp.float32
I32 = jnp.int32
_MXU_DTYPE = jnp.bfloat16

D_MODEL = 1024
HEAD_DIM = 64
LANES = 128
POOL_WINDOWS = (2, 4, 8, 16)
POOL_GC = 128
N_HEADS = 8
IDX_HEADS = 8
IDX_DIM = 32
DSA_TOPK = 256
C_KV_GROUPS = 2
HEADS_PER_GROUP = N_HEADS // C_KV_GROUPS
CMP_LEN = 32
CMP_STRIDE = 16
SLC_LEN = 64
SLC_N = 16
WIN = 512
FORCE_BONUS = 1e4
LN_EPS = 1e-5
NEG = -1e30
QBLK = 128
KCHUNK = 512
VMEM_LIMIT = 56 * 1024 * 1024

_IN_WIDTHS = (512, 512, 512, 64, 64, 512, 256, 32, 8, 512, 128, 128, 128, 128, 128, 128, 24, 512, 3072)
_IN_NAMES = ("a_x", "a_z", "b_q", "b_k", "b_v", "b_z", "i_q", "i_k", "i_w", "c_q", "c_kc", "c_vc",
             "c_ks", "c_vs", "c_kw", "c_vw", "c_g", "c_z", "g_merge")
_N_IN = sum(_IN_WIDTHS)
_OFF = dict(zip(_IN_NAMES, np.cumsum((0,) + _IN_WIDTHS[:-1])))
_WID = dict(zip(_IN_NAMES, _IN_WIDTHS))


def _seg(name, lo=0, hi=None):
    hi = _WID[name] if hi is None else hi
    return np.arange(_OFF[name] + lo, _OFF[name] + hi)


def _pad(n):
    return np.full((n,), _N_IN)


def _layout32():
    segs, off, pos = [], {}, 0

    def add(name, idx):
        nonlocal pos
        off[name] = pos
        segs.append(idx)
        pos += len(idx)

    add("a_x", _seg("a_x"))
    add("a_z", _seg("a_z"))
    add("b_z", _seg("b_z"))
    add("c_z", _seg("c_z"))
    add("g_merge", _seg("g_merge"))
    add("c_kc", _seg("c_kc"))
    add("c_vc", _seg("c_vc"))
    add("i_w", np.concatenate([_seg("i_w"), _pad(LANES - 8)]))
    for g in range(C_KV_GROUPS):
        add(f"c_g{g}", np.concatenate([_seg("c_g", 12 * g, 12 * g + 12), _pad(LANES - 12)]))
    idx = np.concatenate(segs)
    return idx, np.ones((len(idx),), np.float32), off


def _layout16():
    segs, scales, off, pos = [], [], {}, 0

    def add(name, idx, scale=1.0):
        nonlocal pos
        off[name] = pos
        segs.append(idx)
        scales.append(np.full((len(idx),), scale, np.float32))
        pos += len(idx)

    qk_scale = HEAD_DIM ** -0.5 * float(np.log2(np.e))
    add("b_q", _seg("b_q"), qk_scale)
    add("c_q", _seg("c_q"), qk_scale)
    add("i_q", _seg("i_q"))
    for name in ("c_ks", "c_kw"):
        for g in range(C_KV_GROUPS):
            one = _seg(name, HEAD_DIM * g, HEAD_DIM * (g + 1))
            add(f"{name}{g}", np.concatenate([one, one]))
    add("b_k", np.concatenate([_seg("b_k")] * 2))
    add("i_k", np.concatenate([_seg("i_k")] * (LANES // IDX_DIM)))
    add("c_vs", _seg("c_vs"))
    add("c_vw", _seg("c_vw"))
    add("b_v", np.concatenate([_seg("b_v"), _pad(LANES - HEAD_DIM)]))
    add("pad", _pad(LANES))
    return np.concatenate(segs), np.concatenate(scales), off


def _gather_cols(a, idx):
    pieces, i = [], 0
    while i < len(idx):
        j = i + 1
        if idx[i] == _N_IN:
            while j < len(idx) and idx[j] == _N_IN:
                j += 1
            pieces.append(jnp.zeros(a.shape[:-1] + (j - i,), a.dtype))
        else:
            while j < len(idx) and idx[j] == idx[j - 1] + 1:
                j += 1
            pieces.append(a[..., int(idx[i]):int(idx[i]) + (j - i)])
        i = j
    return jnp.concatenate(pieces, axis=-1)


_IDX32, _SCALE32, _OFF32 = _layout32()
_IDX16, _SCALE16, _OFF16 = _layout16()
_N32 = len(_IDX32)
_N16 = len(_IDX16)
_TN32 = 640
_TN16 = 512
assert _N32 % _TN32 == 0 and _N16 % _TN16 == 0


def _sigmoid(x):
    return 1.0 / (1.0 + jnp.exp(-x))


def _dot(a, b):
    return jnp.dot(a, b, preferred_element_type=F32)


def _mm_bias_kernel(x_ref, w_ref, b_ref, o_ref):
    o_ref[...] = (_dot(x_ref[...], w_ref[...]) + b_ref[...]).astype(o_ref.dtype)


def _matmul_bias(x, w, b, out_dtype, tm, tn, name):
    m, k = x.shape
    n = w.shape[1]
    return pl.pallas_call(
        _mm_bias_kernel,
        grid=(m // tm, n // tn),
        in_specs=[pl.BlockSpec((tm, k), lambda i, j: (i, 0)),
                  pl.BlockSpec((k, tn), lambda i, j: (0, j)),
                  pl.BlockSpec((1, tn), lambda i, j: (0, j))],
        out_specs=pl.BlockSpec((tm, tn), lambda i, j: (i, j)),
        out_shape=jax.ShapeDtypeStruct((m, n), out_dtype),
        compiler_params=pltpu.CompilerParams(dimension_semantics=("arbitrary", "arbitrary"),
                                             vmem_limit_bytes=VMEM_LIMIT),
        name=name,
    )(x, w, b)


_HALO = 16


def _pool_kernel(xa_ref, halo_ref, az_ref, pw_ref, pb_ref, ps_ref, o_ref, *, tb):
    i = pl.program_id(1)
    cur = xa_ref[0]
    halo = jnp.where(i > 0, halo_ref[0], 0.0)
    ext = jnp.concatenate([halo, cur], axis=0)
    pos = (i * tb + 1 + lax.broadcasted_iota(I32, (tb, 1), 0)).astype(F32)
    outs = []
    for g, wnd in enumerate(POOL_WINDOWS):
        s = ext[:, g * POOL_GC:(g + 1) * POOL_GC]
        k = 1
        while k < wnd:
            s = s + pltpu.roll(s, k, axis=0)
            k *= 2
        mean = s[_HALO:] / jnp.minimum(pos, float(wnd))
        pooled = mean - cur[:, g * POOL_GC:(g + 1) * POOL_GC]
        outs.append(_dot(pooled.astype(_MXU_DTYPE), pw_ref[g]))
    y = jnp.concatenate(outs, axis=1) + pb_ref[...]
    az = az_ref[0]
    o_ref[0] = (y * ps_ref[...] * (az * _sigmoid(az))).astype(o_ref.dtype)


def _pool_mixer(u32, pool_w, pool_b, pool_scale, tb=512):
    b, s, _ = u32.shape
    hb = tb // _HALO
    return pl.pallas_call(
        functools.partial(_pool_kernel, tb=tb),
        grid=(b, s // tb),
        in_specs=[pl.BlockSpec((1, tb, 512), lambda bi, i: (bi, i, _OFF32["a_x"] // 512)),
                  pl.BlockSpec((1, _HALO, 512), lambda bi, i: (bi, jnp.maximum(i * hb - 1, 0), _OFF32["a_x"] // 512)),
                  pl.BlockSpec((1, tb, 512), lambda bi, i: (bi, i, _OFF32["a_z"] // 512)),
                  pl.BlockSpec((4, POOL_GC, POOL_GC), lambda bi, i: (0, 0, 0)),
                  pl.BlockSpec((1, 512), lambda bi, i: (0, 0)),
                  pl.BlockSpec((1, 512), lambda bi, i: (0, 0))],
        out_specs=pl.BlockSpec((1, tb, 512), lambda bi, i: (bi, i, 0)),
        out_shape=jax.ShapeDtypeStruct((b, s, 512), _MXU_DTYPE),
        compiler_params=pltpu.CompilerParams(dimension_semantics=("arbitrary", "arbitrary"),
                                             vmem_limit_bytes=VMEM_LIMIT),
        name="pool_mixer",
    )(u32, u32, u32, pool_w, pool_b, pool_scale)


def _compress_kernel(ch_ref, pos_ref, w1_ref, w2_ref, o_ref):
    ch = ch_ref[0, 0]
    pos = pos_ref[0]
    w1 = w1_ref[0]
    half = ch.shape[1]
    n = ch.shape[0]
    a = _dot((ch + pos[0:1]).astype(_MXU_DTYPE), w1[:half])
    bb = _dot((ch + pos[1:2]).astype(_MXU_DTYPE), w1[half:])
    h = a + pltpu.roll(bb, n - 1, axis=0)
    h = h * _sigmoid(h)
    o_ref[0, 0] = _dot(h.astype(_MXU_DTYPE), w2_ref[0]).astype(o_ref.dtype)


def _compress(chunks, pos, w1, w2dup):
    _, bg, n, width = chunks.shape
    return pl.pallas_call(
        _compress_kernel,
        grid=(2, bg),
        in_specs=[pl.BlockSpec((1, 1, n, width), lambda kv, i: (kv, i, 0, 0)),
                  pl.BlockSpec((1, 2, width), lambda kv, i: (kv, 0, 0)),
                  pl.BlockSpec((1, 2 * width, HEAD_DIM), lambda kv, i: (kv, 0, 0)),
                  pl.BlockSpec((1, HEAD_DIM, LANES), lambda kv, i: (kv, 0, 0))],
        out_specs=pl.BlockSpec((1, 1, n, LANES), lambda kv, i: (kv, i, 0, 0)),
        out_shape=jax.ShapeDtypeStruct((2, bg, n, LANES), _MXU_DTYPE),
        compiler_params=pltpu.CompilerParams(dimension_semantics=("arbitrary", "arbitrary"),
                                             vmem_limit_bytes=VMEM_LIMIT),
        name="nsa_compress",
    )(chunks, pos, w1, w2dup)


KTILE = 128


def _heads_t(qf, n_tiles, width):
    lane = lax.broadcasted_iota(I32, (1, LANES), 1)
    per_tile = LANES // width
    out = []
    for tix in range(n_tiles):
        qt = qf[:, tix * LANES:(tix + 1) * LANES]
        for j in range(per_tile):
            keep = (lane >= j * width) & (lane < (j + 1) * width)
            out.append(jnp.where(keep, qt, 0.0).T.astype(_MXU_DTYPE))
    return jnp.concatenate(out, axis=1)


def _init_state(cols, rows):
    return (jnp.full((1, cols), -1e38, F32), jnp.zeros((1, cols), F32), jnp.zeros((rows, cols), F32))


def _attend(k_blk, vt_blk, q_all, bias, state):
    m, l, acc = state
    s = _dot(k_blk, q_all) + jnp.concatenate([bias] * (q_all.shape[1] // LANES), axis=1)
    m_new = jnp.maximum(m, jnp.max(s, axis=0, keepdims=True))
    a = jnp.exp2(m - m_new)
    p = jnp.exp2(s - m_new)
    l = a * l + jnp.sum(p, axis=0, keepdims=True)
    acc = a * acc + _dot(vt_blk, p.astype(_MXU_DTYPE))
    return m_new, l, acc


def _finish(state):
    _, l, acc = state
    return acc / l


def _kv_tiles(v, tile):
    b, s, w = v.shape
    c = w // HEAD_DIM
    return v.reshape(b, s // tile, tile, c, HEAD_DIM).transpose(0, 3, 1, 4, 2)


def _nsa_kernel(q_ref, kc_ref, cv_ref, ks_ref, vs_ref, kw_ref, vw_ref, cg_ref, cz_ref, o_ref, selb_ref,
                *, ck, win, n_sel):
    qi = pl.program_id(2)
    q0 = qi * QBLK
    hpg = HEADS_PER_GROUP
    cols = hpg * QBLK
    tq = q0 + lax.broadcasted_iota(I32, (1, QBLK), 1)
    q_all = _heads_t(q_ref[0].astype(F32), hpg // 2, HEAD_DIM)

    ncp = kc_ref.shape[2]
    n_s = selb_ref.shape[0]
    cend = lax.broadcasted_iota(I32, (ncp, 1), 0) * CMP_STRIDE + (CMP_LEN - 1)
    out = _finish(_attend(kc_ref[0, 0], cv_ref[0], q_all, jnp.where(cend <= tq, 0.0, NEG),
                          _init_state(cols, HEAD_DIM + n_s)))
    seen = jnp.concatenate([tq >= CMP_LEN - 1] * hpg, axis=1)
    out = jnp.where(seen, out, 0.0)
    o_c = out[:HEAD_DIM]
    work = out[HEAD_DIM:, 0:QBLK]
    for r in range(1, hpg):
        work = work + out[HEAD_DIM:, r * QBLK:(r + 1) * QBLK]

    ji = lax.broadcasted_iota(I32, (n_s, 1), 0)
    jf = ji.astype(F32)
    blk = lax.shift_right_logical(tq, 6)
    forced = (ji == 0) | (ji == blk) | (ji == blk - 1)
    work = work + jnp.where(forced, FORCE_BONUS, 0.0)
    work = jnp.where(ji * SLC_LEN <= tq, work, NEG)
    sel = jnp.zeros((n_s, QBLK), F32)
    for _ in range(n_sel):
        mx = jnp.max(work, axis=0, keepdims=True)
        first = jnp.min(jnp.where(work == mx, jf, float(n_s)), axis=0, keepdims=True)
        hit = jf == first
        sel = jnp.where(hit, 1.0, sel)
        work = jnp.where(hit, -3e38, work)
    selb_ref[...] = jnp.where(sel > 0.5, 0.0, NEG)

    per_chunk = ck // SLC_LEN
    krow = lax.broadcasted_iota(I32, (ck, 1), 0)

    def sel_body(c, state):
        off = pl.multiple_of(c * ck, ck)
        rows = [jnp.broadcast_to(selb_ref[pl.ds(c * per_chunk + i, 1), :], (SLC_LEN, QBLK))
                for i in range(per_chunk)]
        bias = jnp.where(off + krow <= tq, jnp.concatenate(rows, axis=0), NEG)
        return _attend(ks_ref[0, pl.ds(off, ck), :], vs_ref[0, 0, c], q_all, bias, state)

    o_s = _finish(lax.fori_loop(0, q0 // ck + 1, sel_body, _init_state(cols, HEAD_DIM)))

    span = win + QBLK
    start = pl.multiple_of(jnp.maximum(q0 - win, 0), QBLK)
    kpos = start + lax.broadcasted_iota(I32, (span, 1), 0)
    bias = jnp.where((kpos <= tq) & (kpos > tq - win), 0.0, NEG)
    vw = jnp.concatenate([vw_ref[0, 0, start // KTILE + i] for i in range(span // KTILE)], axis=1)
    o_w = _finish(_attend(kw_ref[0, pl.ds(start, span), :], vw, q_all, bias, _init_state(cols, HEAD_DIM)))

    gate = _sigmoid(cg_ref[0].T[0:16])
    ys = []
    for r in range(hpg):
        c_ = slice(r * QBLK, (r + 1) * QBLK)
        ys.append(gate[3 * r:3 * r + 1] * o_c[:, c_] + gate[3 * r + 1:3 * r + 2] * o_s[:, c_]
                  + gate[3 * r + 2:3 * r + 3] * o_w[:, c_])
    tiles = [jnp.concatenate(ys[i:i + 2], axis=0).T for i in range(0, hpg, 2)]
    cz = cz_ref[0]
    o_ref[0] = (jnp.concatenate(tiles, axis=1) * (cz * _sigmoid(cz))).astype(o_ref.dtype)


def _nsa_mixer(u16, u32, kvcmp, overlap_t):
    b, s, _ = u16.shape
    ck = min(KCHUNK, s)
    gw = HEADS_PER_GROUP * HEAD_DIM
    bg, ncp = kvcmp.shape[1], kvcmp.shape[2]
    n_s = overlap_t.shape[0]
    vs_t = _kv_tiles(u16[:, :, _OFF16["c_vs"]:_OFF16["c_vs"] + LANES], ck)
    vw_t = _kv_tiles(u16[:, :, _OFF16["c_vw"]:_OFF16["c_vw"] + LANES], KTILE)
    vc_t = kvcmp[1][:, :, :HEAD_DIM].transpose(0, 2, 1)
    cval = jnp.concatenate([vc_t, jnp.broadcast_to(overlap_t[None], (bg, n_s, ncp))], axis=1)
    rows_c = HEAD_DIM + n_s

    def k_spec(name):
        base = _OFF16[name + "0"] // LANES
        return pl.BlockSpec((1, s, LANES), lambda bi, g, qi: (bi, 0, base + g))

    def vt_spec(tile):
        return pl.BlockSpec((1, 1, s // tile, HEAD_DIM, tile), lambda bi, g, qi: (bi, g, 0, 0, 0))

    return pl.pallas_call(
        functools.partial(_nsa_kernel, ck=ck, win=WIN, n_sel=min(SLC_N, n_s)),
        grid=(b, C_KV_GROUPS, s // QBLK),
        in_specs=[pl.BlockSpec((1, QBLK, gw), lambda bi, g, qi: (bi, qi, _OFF16["c_q"] // gw + g)),
                  pl.BlockSpec((1, 1, ncp, LANES), lambda bi, g, qi: (0, bi * C_KV_GROUPS + g, 0, 0)),
                  pl.BlockSpec((1, rows_c, ncp), lambda bi, g, qi: (bi * C_KV_GROUPS + g, 0, 0)),
                  k_spec("c_ks"), vt_spec(ck), k_spec("c_kw"), vt_spec(KTILE),
                  pl.BlockSpec((1, QBLK, LANES), lambda bi, g, qi: (bi, qi, _OFF32["c_g0"] // LANES + g)),
                  pl.BlockSpec((1, QBLK, gw), lambda bi, g, qi: (bi, qi, _OFF32["c_z"] // gw + g))],
        out_specs=pl.BlockSpec((1, QBLK, gw), lambda bi, g, qi: (bi, qi, g)),
        out_shape=jax.ShapeDtypeStruct((b, s, N_HEADS * HEAD_DIM), _MXU_DTYPE),
        scratch_shapes=[pltpu.VMEM((n_s, QBLK), F32)],
        compiler_params=pltpu.CompilerParams(dimension_semantics=("arbitrary", "arbitrary", "arbitrary"),
                                             vmem_limit_bytes=VMEM_LIMIT),
        name="nsa_mixer",
    )(u16, kvcmp, cval, u16, vs_t, u16, vw_t, u32, u32)


def _dsa_kernel(q_ref, iq_ref, k_ref, v_ref, ik_ref, iw_ref, bz_ref, o_ref, keys_ref, bias_ref, cut_ref,
                *, ck, topk, nbits):
    qi = pl.program_id(1)
    q0 = qi * QBLK
    nch = q0 // ck + 1
    tq = q0 + lax.broadcasted_iota(I32, (1, QBLK), 1)
    krow = lax.broadcasted_iota(I32, (ck, 1), 0)
    srow = lax.broadcasted_iota(I32, (8, 1), 0)

    iq_all = _heads_t(iq_ref[0].astype(F32), IDX_HEADS * IDX_DIM // LANES, IDX_DIM)
    iw_t = (iw_ref[0] * (IDX_HEADS ** -0.5 * IDX_DIM ** -0.5)).T

    def score_body(c, _):
        off = pl.multiple_of(c * ck, ck)
        rel = _dot(ik_ref[0, pl.ds(off, ck), :], iq_all)
        sc = jnp.maximum(rel[:, 0:QBLK], 0.0) * iw_t[0:1]
        for h in range(1, IDX_HEADS):
            sc = sc + jnp.maximum(rel[:, h * QBLK:(h + 1) * QBLK], 0.0) * iw_t[h:h + 1]
        sc = jnp.where(off + krow <= tq, sc, NEG)
        sc = jnp.where(sc == 0.0, 0.0, sc)
        bits = pltpu.bitcast(sc, I32)
        keys_ref[c] = bits ^ (lax.shift_right_arithmetic(bits, 31) & 0x7FFFFFFF)
        return 0

    lax.fori_loop(0, nch, score_body, 0)

    n_acc = 4

    def count(pred):
        def body(c, accs):
            accs = list(accs)
            for r in range(ck // 8):
                k = keys_ref[c, r * 8:(r + 1) * 8, :]
                accs[r % n_acc] = accs[r % n_acc] + jnp.where(pred(k, c * ck + r * 8 + srow), 1.0, 0.0)
            return tuple(accs)
        accs = lax.fori_loop(0, nch, body, tuple(jnp.zeros((8, LANES), F32) for _ in range(n_acc)))
        return jnp.sum(sum(accs[1:], accs[0]), axis=0, keepdims=True)

    def bit_body(i, thr):
        cand = thr + lax.shift_left(jnp.int32(1), 31 - i)
        return jnp.where(count(lambda k, _: k >= cand) >= topk, cand, thr)

    thr = lax.fori_loop(0, 32, bit_body, jnp.full((1, LANES), -2 ** 31, I32))
    c_gt = count(lambda k, _: k > thr)
    c_eq = count(lambda k, _: k >= thr) - c_gt
    need = topk - c_gt

    cut_ref[...] = jnp.full(cut_ref.shape, 2 ** nbits, I32)

    @pl.when(jnp.max(jnp.where(c_eq > need, 1.0, 0.0)) > 0.0)
    def _():
        def tie_body(i, cut):
            cand = cut + lax.shift_left(jnp.int32(1), nbits - 1 - i)
            below = count(lambda k, kpos: (k == thr) & (kpos < cand))
            return jnp.where(below < need, cand, cut)
        cut = lax.fori_loop(0, nbits, tie_body, jnp.zeros((1, LANES), I32))
        cut_ref[...] = jnp.broadcast_to(cut, cut_ref.shape)

    cut = cut_ref[0:1, :]

    def bias_body(c, _):
        k = keys_ref[c]
        kpos = c * ck + krow
        chosen = (k > thr) | ((k == thr) & (kpos <= cut))
        bias_ref[c] = jnp.where(chosen & (kpos <= tq), 0.0, NEG)
        return 0

    lax.fori_loop(0, nch, bias_body, 0)

    q_all = _heads_t(q_ref[0].astype(F32), N_HEADS * HEAD_DIM // LANES, HEAD_DIM)

    def att_body(c, state):
        off = pl.multiple_of(c * ck, ck)
        return _attend(k_ref[0, pl.ds(off, ck), :], v_ref[0, 0, c], q_all, bias_ref[c], state)

    out = _finish(lax.fori_loop(0, nch, att_body, _init_state(N_HEADS * QBLK, HEAD_DIM)))
    tiles = [jnp.concatenate([out[:, h * QBLK:(h + 1) * QBLK], out[:, (h + 1) * QBLK:(h + 2) * QBLK]], axis=0).T
             for h in range(0, N_HEADS, 2)]
    bz = bz_ref[0]
    o_ref[0] = (jnp.concatenate(tiles, axis=1) * (bz * _sigmoid(bz))).astype(o_ref.dtype)


def _dsa_mixer(u16, u32):
    b, s, _ = u16.shape
    ck = min(KCHUNK, s)
    w = N_HEADS * HEAD_DIM
    iqw = IDX_HEADS * IDX_DIM
    v_t = _kv_tiles(u16[:, :, _OFF16["b_v"]:_OFF16["b_v"] + HEAD_DIM], ck)

    def k_spec(name):
        return pl.BlockSpec((1, s, LANES), lambda bi, qi: (bi, 0, _OFF16[name] // LANES))

    return pl.pallas_call(
        functools.partial(_dsa_kernel, ck=ck, topk=min(DSA_TOPK, s // 4), nbits=int(s).bit_length()),
        grid=(b, s // QBLK),
        in_specs=[pl.BlockSpec((1, QBLK, w), lambda bi, qi: (bi, qi, _OFF16["b_q"] // w)),
                  pl.BlockSpec((1, QBLK, iqw), lambda bi, qi: (bi, qi, _OFF16["i_q"] // iqw)),
                  k_spec("b_k"),
                  pl.BlockSpec((1, 1, s // ck, HEAD_DIM, ck), lambda bi, qi: (bi, 0, 0, 0, 0)),
                  k_spec("i_k"),
                  pl.BlockSpec((1, QBLK, LANES), lambda bi, qi: (bi, qi, _OFF32["i_w"] // LANES)),
                  pl.BlockSpec((1, QBLK, w), lambda bi, qi: (bi, qi, _OFF32["b_z"] // w))],
        out_specs=pl.BlockSpec((1, QBLK, w), lambda bi, qi: (bi, qi, 0)),
        out_shape=jax.ShapeDtypeStruct((b, s, w), _MXU_DTYPE),
        scratch_shapes=[pltpu.VMEM((s // ck, ck, QBLK), I32),
                        pltpu.VMEM((s // ck, ck, QBLK), F32),
                        pltpu.VMEM((8, LANES), I32)],
        compiler_params=pltpu.CompilerParams(dimension_semantics=("arbitrary", "arbitrary"),
                                             vmem_limit_bytes=VMEM_LIMIT),
        name="dsa_mixer",
    )(u16, u16, u16, v_t, u16, u32, u32)


def _merge_kernel(x_ref, ya_ref, yb_ref, yc_ref, g0_ref, g1_ref, g2_ref, wa_ref, wb_ref, wc_ref, wo_ref,
                  lg_ref, lb_ref, o32_ref, o16_ref, *, alpha):
    m = (_sigmoid(g0_ref[...]) * _dot(ya_ref[...], wa_ref[...])
         + _sigmoid(g1_ref[...]) * _dot(yb_ref[...], wb_ref[...])
         + _sigmoid(g2_ref[...]) * _dot(yc_ref[...], wc_ref[...]))
    z = alpha * x_ref[...] + _dot(m.astype(_MXU_DTYPE), wo_ref[...])
    mu = jnp.mean(z, axis=1, keepdims=True)
    zc = z - mu
    var = jnp.mean(zc * zc, axis=1, keepdims=True)
    y = zc * lax.rsqrt(var + LN_EPS) * lg_ref[...] + lb_ref[...]
    o32_ref[...] = y
    o16_ref[...] = y.astype(o16_ref.dtype)


def _merge(x, ya, yb, yc, u32, wa, wb, wc, wo, lg, lb, alpha, tm=512):
    m, d = x.shape
    w = ya.shape[1]
    gbase = _OFF32["g_merge"] // d
    row = lambda i: (i, 0)
    const = lambda i: (0, 0)
    return pl.pallas_call(
        functools.partial(_merge_kernel, alpha=alpha),
        grid=(m // tm,),
        in_specs=[pl.BlockSpec((tm, d), row),
                  pl.BlockSpec((tm, w), row), pl.BlockSpec((tm, w), row), pl.BlockSpec((tm, w), row),
                  pl.BlockSpec((tm, d), lambda i: (i, gbase)),
                  pl.BlockSpec((tm, d), lambda i: (i, gbase + 1)),
                  pl.BlockSpec((tm, d), lambda i: (i, gbase + 2)),
                  pl.BlockSpec((w, d), const), pl.BlockSpec((w, d), const), pl.BlockSpec((w, d), const),
                  pl.BlockSpec((d, d), const), pl.BlockSpec((1, d), const), pl.BlockSpec((1, d), const)],
        out_specs=[pl.BlockSpec((tm, d), row), pl.BlockSpec((tm, d), row)],
        out_shape=[jax.ShapeDtypeStruct((m, d), F32), jax.ShapeDtypeStruct((m, d), _MXU_DTYPE)],
        compiler_params=pltpu.CompilerParams(dimension_semantics=("arbitrary",), vmem_limit_bytes=VMEM_LIMIT),
        name="merge_out_ln",
    )(x, ya, yb, yc, u32, u32, u32, wa, wb, wc, wo, lg, lb)


def _overlap_matrix(ncp, s):
    n_c = (s - CMP_LEN) // CMP_STRIDE + 1
    c_start = np.arange(ncp) * CMP_STRIDE
    s_start = np.arange(s // SLC_LEN) * SLC_LEN
    ov = (c_start[:, None] <= s_start[None, :] + SLC_LEN - 1) & (c_start[:, None] + CMP_LEN - 1 >= s_start[None, :])
    ov &= (np.arange(ncp) < n_c)[:, None]
    return ov.astype(np.float32)


def _layer(x32, x16, p, consts, alpha):
    b, s, d = x32.shape
    m = b * s
    x16 = x16.reshape(m, d)
    u32 = _matmul_bias(x16, p["w32"], p["b32"], F32, min(1024, m), _TN32, "in_proj_f32").reshape(b, s, _N32)
    u16 = _matmul_bias(x16, p["w16"], p["b16"], _MXU_DTYPE, min(1024, m), _TN16, "in_proj_bf16").reshape(b, s, _N16)

    y_a = _pool_mixer(u32, p["pool_w"], p["pool_b"], p["pool_scale"])

    nch16 = s // CMP_STRIDE
    tok = u32[:, :, _OFF32["c_kc"]:_OFF32["c_kc"] + 2 * LANES]
    chunks = (tok.reshape(b, nch16, CMP_STRIDE, 2, C_KV_GROUPS, HEAD_DIM)
              .transpose(3, 0, 4, 1, 2, 5).reshape(2, b * C_KV_GROUPS, nch16, CMP_STRIDE * HEAD_DIM))
    kvcmp = _compress(chunks, p["cmp_pos"], p["cmp_w1"], p["cmp_w2"])

    y_c = _nsa_mixer(u16, u32, kvcmp, consts["overlap_t"])
    y_b = _dsa_mixer(u16, u32)

    x32n, x16n = _merge(x32.reshape(m, d), y_a.reshape(m, -1), y_b.reshape(m, -1), y_c.reshape(m, -1),
                        u32.reshape(m, _N32), p["w_pa"], p["w_pb"], p["w_pc"], p["w_o"], p["ln_g"], p["ln_b"], alpha)
    return x32n.reshape(b, s, d), x16n.reshape(b, s, d)


def _prepare_params(w_in, b_in, pool_w, pool_b, pool_scale, cmp_pos_k, cmp_pos_v, cmp_w1_k, cmp_w2_k,
                    cmp_w1_v, cmp_w2_v, w_proj_a, w_proj_b, w_proj_c, w_o, ln_g, ln_b):
    nl = w_in.shape[0]
    mx = _MXU_DTYPE
    half = CMP_LEN // 2
    pos = jnp.stack([cmp_pos_k, cmp_pos_v], axis=1).reshape(nl, 2, 2, half * HEAD_DIM)
    return {
        "w32": (_gather_cols(w_in, _IDX32) * _SCALE32).astype(mx),
        "b32": (_gather_cols(b_in, _IDX32) * _SCALE32)[:, None, :],
        "w16": (_gather_cols(w_in, _IDX16) * _SCALE16).astype(mx),
        "b16": (_gather_cols(b_in, _IDX16) * _SCALE16)[:, None, :],
        "pool_w": pool_w.astype(mx),
        "pool_b": pool_b.reshape(nl, 1, -1),
        "pool_scale": pool_scale.reshape(nl, 1, -1),
        "cmp_pos": pos,
        "cmp_w1": jnp.stack([cmp_w1_k, cmp_w1_v], axis=1).astype(mx),
        "cmp_w2": jnp.concatenate([jnp.stack([cmp_w2_k, cmp_w2_v], axis=1)] * 2, axis=-1).astype(mx),
        "w_pa": w_proj_a.astype(mx), "w_pb": w_proj_b.astype(mx), "w_pc": w_proj_c.astype(mx),
        "w_o": w_o.astype(mx),
        "ln_g": ln_g[:, None, :], "ln_b": ln_b[:, None, :],
    }


def kernel(x, w_in, b_in, pool_w, pool_b, pool_scale, cmp_pos_k, cmp_pos_v, cmp_w1_k, cmp_w2_k, cmp_w1_v, cmp_w2_v, w_proj_a, w_proj_b, w_proj_c, w_o, ln_g, ln_b):
    depth = w_in.shape[0]
    s = x.shape[1]
    params = _prepare_params(w_in, b_in, pool_w, pool_b, pool_scale, cmp_pos_k, cmp_pos_v, cmp_w1_k, cmp_w2_k,
                             cmp_w1_v, cmp_w2_v, w_proj_a, w_proj_b, w_proj_c, w_o, ln_g, ln_b)
    consts = {"overlap_t": jnp.asarray(_overlap_matrix(s // CMP_STRIDE, s).T, _MXU_DTYPE)}
    alpha = (2 * depth) ** 0.25
    h32, h16 = x, x.astype(_MXU_DTYPE)
    for l in range(depth):
        h32, h16 = _layer(h32, h16, {k: v[l] for k, v in params.items()}, consts, alpha)
    return h32
```

```python
import functools

import numpy as np
import jax
import jax.numpy as jnp
from jax import lax
from jax.experimental import pallas as pl
from jax.experimental.pallas import tpu as pltpu

F32 = jnp.float32
I32 = jnp.int32
_MXU_DTYPE = jnp.bfloat16

D_MODEL = 1024
HEAD_DIM = 64
LANES = 128
POOL_WINDOWS = (2, 4, 8, 16)
POOL_GC = 128
N_HEADS = 8
IDX_HEADS = 8
IDX_DIM = 32
DSA_TOPK = 256
C_KV_GROUPS = 2
HEADS_PER_GROUP = N_HEADS // C_KV_GROUPS
CMP_LEN = 32
CMP_STRIDE = 16
SLC_LEN = 64
SLC_N = 16
WIN = 512
FORCE_BONUS = 1e4
LN_EPS = 1e-5
NEG = -1e30
QBLK = 128
KCHUNK = 512
VMEM_LIMIT = 56 * 1024 * 1024

_IN_WIDTHS = (512, 512, 512, 64, 64, 512, 256, 32, 8, 512, 128, 128, 128, 128, 128, 128, 24, 512, 3072)
_IN_NAMES = ("a_x", "a_z", "b_q", "b_k", "b_v", "b_z", "i_q", "i_k", "i_w", "c_q", "c_kc", "c_vc",
             "c_ks", "c_vs", "c_kw", "c_vw", "c_g", "c_z", "g_merge")
_N_IN = sum(_IN_WIDTHS)
_OFF = dict(zip(_IN_NAMES, np.cumsum((0,) + _IN_WIDTHS[:-1])))
_WID = dict(zip(_IN_NAMES, _IN_WIDTHS))


def _seg(name, lo=0, hi=None):
    hi = _WID[name] if hi is None else hi
    return np.arange(_OFF[name] + lo, _OFF[name] + hi)


def _pad(n):
    return np.full((n,), _N_IN)


def _layout32():
    segs, off, pos = [], {}, 0

    def add(name, idx):
        nonlocal pos
        off[name] = pos
        segs.append(idx)
        pos += len(idx)

    add("a_x", _seg("a_x"))
    add("a_z", _seg("a_z"))
    add("b_z", _seg("b_z"))
    add("c_z", _seg("c_z"))
    add("g_merge", _seg("g_merge"))
    add("c_kc", _seg("c_kc"))
    add("c_vc", _seg("c_vc"))
    add("i_w", np.concatenate([_seg("i_w"), _pad(LANES - 8)]))
    for g in range(C_KV_GROUPS):
        add(f"c_g{g}", np.concatenate([_seg("c_g", 12 * g, 12 * g + 12), _pad(LANES - 12)]))
    idx = np.concatenate(segs)
    return idx, np.ones((len(idx),), np.float32), off


def _layout16():
    segs, scales, off, pos = [], [], {}, 0

    def add(name, idx, scale=1.0):
        nonlocal pos
        off[name] = pos
        segs.append(idx)
        scales.append(np.full((len(idx),), scale, np.float32))
        pos += len(idx)

    qk_scale = HEAD_DIM ** -0.5 * float(np.log2(np.e))
    add("b_q", _seg("b_q"), qk_scale)
    add("c_q", _seg("c_q"), qk_scale)
    add("i_q", _seg("i_q"))
    for name in ("c_ks", "c_kw"):
        for g in range(C_KV_GROUPS):
            one = _seg(name, HEAD_DIM * g, HEAD_DIM * (g + 1))
            add(f"{name}{g}", np.concatenate([one, one]))
    add("b_k", np.concatenate([_seg("b_k")] * 2))
    add("i_k", np.concatenate([_seg("i_k")] * (LANES // IDX_DIM)))
    add("c_vs", _seg("c_vs"))
    add("c_vw", _seg("c_vw"))
    add("b_v", np.concatenate([_seg("b_v"), _pad(LANES - HEAD_DIM)]))
    add("pad", _pad(LANES))
    return np.concatenate(segs), np.concatenate(scales), off


def _gather_cols(a, idx):
    pieces, i = [], 0
    while i < len(idx):
        j = i + 1
        if idx[i] == _N_IN:
            while j < len(idx) and idx[j] == _N_IN:
                j += 1
            pieces.append(jnp.zeros(a.shape[:-1] + (j - i,), a.dtype))
        else:
            while j < len(idx) and idx[j] == idx[j - 1] + 1:
                j += 1
            pieces.append(a[..., int(idx[i]):int(idx[i]) + (j - i)])
        i = j
    return jnp.concatenate(pieces, axis=-1)


_IDX32, _SCALE32, _OFF32 = _layout32()
_IDX16, _SCALE16, _OFF16 = _layout16()
_N32 = len(_IDX32)
_N16 = len(_IDX16)
_TN32 = 640
_TN16 = 512
assert _N32 % _TN32 == 0 and _N16 % _TN16 == 0


def _sigmoid(x):
    return 1.0 / (1.0 + jnp.exp(-x))


def _dot(a, b):
    return jnp.dot(a, b, preferred_element_type=F32)


def _mm_bias_kernel(x_ref, w_ref, b_ref, o_ref):
    o_ref[...] = (_dot(x_ref[...], w_ref[...]) + b_ref[...]).astype(o_ref.dtype)


def _matmul_bias(x, w, b, out_dtype, tm, tn, name):
    m, k = x.shape
    n = w.shape[1]
    return pl.pallas_call(
        _mm_bias_kernel,
        grid=(m // tm, n // tn),
        in_specs=[pl.BlockSpec((tm, k), lambda i, j: (i, 0)),
                  pl.BlockSpec((k, tn), lambda i, j: (0, j)),
                  pl.BlockSpec((1, tn), lambda i, j: (0, j))],
        out_specs=pl.BlockSpec((tm, tn), lambda i, j: (i, j)),
        out_shape=jax.ShapeDtypeStruct((m, n), out_dtype),
        compiler_params=pltpu.CompilerParams(dimension_semantics=("arbitrary", "arbitrary"),
                                             vmem_limit_bytes=VMEM_LIMIT),
        name=name,
    )(x, w, b)


_HALO = 16


def _pool_kernel(xa_ref, halo_ref, az_ref, pw_ref, pb_ref, ps_ref, o_ref, *, tb):
    i = pl.program_id(1)
    cur = xa_ref[0]
    halo = jnp.where(i > 0, halo_ref[0], 0.0)
    ext = jnp.concatenate([halo, cur], axis=0)
    pos = (i * tb + 1 + lax.broadcasted_iota(I32, (tb, 1), 0)).astype(F32)
    outs = []
    for g, wnd in enumerate(POOL_WINDOWS):
        s = ext[:, g * POOL_GC:(g + 1) * POOL_GC]
        k = 1
        while k < wnd:
            s = s + pltpu.roll(s, k, axis=0)
            k *= 2
        mean = s[_HALO:] / jnp.minimum(pos, float(wnd))
        pooled = mean - cur[:, g * POOL_GC:(g + 1) * POOL_GC]
        outs.append(_dot(pooled.astype(_MXU_DTYPE), pw_ref[g]))
    y = jnp.concatenate(outs, axis=1) + pb_ref[...]
    az = az_ref[0]
    o_ref[0] = (y * ps_ref[...] * (az * _sigmoid(az))).astype(o_ref.dtype)


def _pool_mixer(u32, pool_w, pool_b, pool_scale, tb=512):
    b, s, _ = u32.shape
    hb = tb // _HALO
    return pl.pallas_call(
        functools.partial(_pool_kernel, tb=tb),
        grid=(b, s // tb),
        in_specs=[pl.BlockSpec((1, tb, 512), lambda bi, i: (bi, i, _OFF32["a_x"] // 512)),
                  pl.BlockSpec((1, _HALO, 512), lambda bi, i: (bi, jnp.maximum(i * hb - 1, 0), _OFF32["a_x"] // 512)),
                  pl.BlockSpec((1, tb, 512), lambda bi, i: (bi, i, _OFF32["a_z"] // 512)),
                  pl.BlockSpec((4, POOL_GC, POOL_GC), lambda bi, i: (0, 0, 0)),
                  pl.BlockSpec((1, 512), lambda bi, i: (0, 0)),
                  pl.BlockSpec((1, 512), lambda bi, i: (0, 0))],
        out_specs=pl.BlockSpec((1, tb, 512), lambda bi, i: (bi, i, 0)),
        out_shape=jax.ShapeDtypeStruct((b, s, 512), _MXU_DTYPE),
        compiler_params=pltpu.CompilerParams(dimension_semantics=("arbitrary", "arbitrary"),
                                             vmem_limit_bytes=VMEM_LIMIT),
        name="pool_mixer",
    )(u32, u32, u32, pool_w, pool_b, pool_scale)


def _compress_kernel(ch_ref, pos_ref, w1_ref, w2_ref, o_ref):
    ch = ch_ref[0, 0]
    pos = pos_ref[0]
    w1 = w1_ref[0]
    half = ch.shape[1]
    n = ch.shape[0]
    a = _dot((ch + pos[0:1]).astype(_MXU_DTYPE), w1[:half])
    bb = _dot((ch + pos[1:2]).astype(_MXU_DTYPE), w1[half:])
    h = a + pltpu.roll(bb, n - 1, axis=0)
    h = h * _sigmoid(h)
    o_ref[0, 0] = _dot(h.astype(_MXU_DTYPE), w2_ref[0]).astype(o_ref.dtype)


def _compress(chunks, pos, w1, w2dup):
    _, bg, n, width = chunks.shape
    return pl.pallas_call(
        _compress_kernel,
        grid=(2, bg),
        in_specs=[pl.BlockSpec((1, 1, n, width), lambda kv, i: (kv, i, 0, 0)),
                  pl.BlockSpec((1, 2, width), lambda kv, i: (kv, 0, 0)),
                  pl.BlockSpec((1, 2 * width, HEAD_DIM), lambda kv, i: (kv, 0, 0)),
                  pl.BlockSpec((1, HEAD_DIM, LANES), lambda kv, i: (kv, 0, 0))],
        out_specs=pl.BlockSpec((1, 1, n, LANES), lambda kv, i: (kv, i, 0, 0)),
        out_shape=jax.ShapeDtypeStruct((2, bg, n, LANES), _MXU_DTYPE),
        compiler_params=pltpu.CompilerParams(dimension_semantics=("arbitrary", "arbitrary"),
                                             vmem_limit_bytes=VMEM_LIMIT),
        name="nsa_compress",
    )(chunks, pos, w1, w2dup)


KTILE = 128


def _heads_t(qf, n_tiles, width):
    lane = lax.broadcasted_iota(I32, (1, LANES), 1)
    per_tile = LANES // width
    out = []
    for tix in range(n_tiles):
        qt = qf[:, tix * LANES:(tix + 1) * LANES]
        for j in range(per_tile):
            keep = (lane >= j * width) & (lane < (j + 1) * width)
            out.append(jnp.where(keep, qt, 0.0).T.astype(_MXU_DTYPE))
    return jnp.concatenate(out, axis=1)


def _init_state(cols, rows):
    return (jnp.full((1, cols), -1e38, F32), jnp.zeros((1, cols), F32), jnp.zeros((rows, cols), F32))


def _with_mask_rows(q_all):
    cols = q_all.shape[1]
    r = lax.broadcasted_iota(I32, (LANES, cols), 0)
    c = lax.broadcasted_iota(I32, (LANES, cols), 1)
    eye = jnp.where((c & (LANES - 1)) == r, 1.0, 0.0).astype(_MXU_DTYPE)
    return jnp.concatenate([q_all, eye], axis=0)


def _update(s, vt_blk, state):
    m, l, acc = state
    m_new = jnp.maximum(m, jnp.max(s, axis=0, keepdims=True))
    a = jnp.exp2(m - m_new)
    p = jnp.exp2(s - m_new)
    l = a * l + jnp.sum(p, axis=0, keepdims=True)
    acc = a * acc + _dot(vt_blk, p.astype(_MXU_DTYPE))
    return m_new, l, acc


def _scores(k_blk, bias, q_aug):
    return _dot(jnp.concatenate([k_blk, bias], axis=1), q_aug)


def _attend(k_blk, vt_blk, q_aug, bias, state):
    return _update(_scores(k_blk, bias, q_aug), vt_blk, state)


def _attend_chunks(k_ref, vt_ref, bias_of, q_aug, nch, ck, state, s_a, s_b):
    last = nch - 1
    trips = (nch + 1) // 2

    def fill(dst, c):
        cc = jnp.minimum(c, last)
        cb = jnp.minimum(c, 2 * trips - 1)
        dst[...] = _scores(k_ref[pl.ds(pl.multiple_of(cc * ck, ck), ck), :], bias_of(cb), q_aug)

    def body(j, state):
        c = 2 * j
        fill(s_b, c + 1)
        state = _update(s_a[...], vt_ref[c], state)
        fill(s_a, c + 2)
        return _update(s_b[...], vt_ref[jnp.minimum(c + 1, last)], state)

    fill(s_a, 0)
    return lax.fori_loop(0, trips, body, state)


def _finish(state):
    _, l, acc = state
    return acc / l


def _kv_tiles(v, tile):
    b, s, w = v.shape
    c = w // HEAD_DIM
    return v.reshape(b, s // tile, tile, c, HEAD_DIM).transpose(0, 3, 1, 4, 2)


def _nsa_kernel(q_ref, kc_ref, cv_ref, ks_ref, vs_ref, kw_ref, vw_ref, cg_ref, cz_ref, o_ref, selb_ref,
                sa_ref, sb_ref, *, ck, win, n_sel):
    qi = pl.program_id(2)
    q0 = qi * QBLK
    hpg = HEADS_PER_GROUP
    cols = hpg * QBLK
    tq = q0 + lax.broadcasted_iota(I32, (1, QBLK), 1)
    q_all = _with_mask_rows(_heads_t(q_ref[0].astype(F32), hpg // 2, HEAD_DIM))

    ncp = kc_ref.shape[2]
    n_s = cv_ref.shape[1] - HEAD_DIM
    cend = lax.broadcasted_iota(I32, (ncp, 1), 0) * CMP_STRIDE + (CMP_LEN - 1)
    out = _finish(_attend(kc_ref[0, 0], cv_ref[0], q_all, jnp.where(cend <= tq, 0.0, NEG).astype(_MXU_DTYPE),
                          _init_state(cols, HEAD_DIM + n_s)))
    seen = jnp.concatenate([tq >= CMP_LEN - 1] * hpg, axis=1)
    out = jnp.where(seen, out, 0.0)
    o_c = out[:HEAD_DIM]
    work = out[HEAD_DIM:, 0:QBLK]
    for r in range(1, hpg):
        work = work + out[HEAD_DIM:, r * QBLK:(r + 1) * QBLK]

    ji = lax.broadcasted_iota(I32, (n_s, 1), 0)
    jf = ji.astype(F32)
    blk = lax.shift_right_logical(tq, 6)
    forced = (ji == 0) | (ji == blk) | (ji == blk - 1)
    work = work + jnp.where(forced, FORCE_BONUS, 0.0)
    work = jnp.where(ji * SLC_LEN <= tq, work, NEG)
    sel = jnp.zeros((n_s, QBLK), F32)
    for _ in range(n_sel):
        mx = jnp.max(work, axis=0, keepdims=True)
        first = jnp.min(jnp.where(work == mx, jf, float(n_s)), axis=0, keepdims=True)
        hit = jf == first
        sel = jnp.where(hit, 1.0, sel)
        work = jnp.where(hit, -3e38, work)
    selb_ref[0:n_s, :] = jnp.where(sel > 0.5, 0.0, NEG)
    selb_ref[n_s:, :] = jnp.full((selb_ref.shape[0] - n_s, QBLK), NEG, F32)

    per_chunk = ck // SLC_LEN
    krow = lax.broadcasted_iota(I32, (ck, 1), 0)

    def sel_bias(c):
        rows = [jnp.broadcast_to(selb_ref[pl.ds(c * per_chunk + i, 1), :], (SLC_LEN, QBLK))
                for i in range(per_chunk)]
        return jnp.where(c * ck + krow <= tq, jnp.concatenate(rows, axis=0), NEG).astype(_MXU_DTYPE)

    o_s = _finish(_attend_chunks(ks_ref.at[0], vs_ref.at[0, 0], sel_bias, q_all, q0 // ck + 1, ck,
                                 _init_state(cols, HEAD_DIM), sa_ref, sb_ref))

    span = win + QBLK
    start = pl.multiple_of(jnp.maximum(q0 - win, 0), QBLK)
    kpos = start + lax.broadcasted_iota(I32, (span, 1), 0)
    bias = jnp.where((kpos <= tq) & (kpos > tq - win), 0.0, NEG).astype(_MXU_DTYPE)
    vw = jnp.concatenate([vw_ref[0, 0, start // KTILE + i] for i in range(span // KTILE)], axis=1)
    o_w = _finish(_attend(kw_ref[0, pl.ds(start, span), :], vw, q_all, bias, _init_state(cols, HEAD_DIM)))

    gate = _sigmoid(cg_ref[0].T[0:16])
    ys = []
    for r in range(hpg):
        c_ = slice(r * QBLK, (r + 1) * QBLK)
        ys.append(gate[3 * r:3 * r + 1] * o_c[:, c_] + gate[3 * r + 1:3 * r + 2] * o_s[:, c_]
                  + gate[3 * r + 2:3 * r + 3] * o_w[:, c_])
    tiles = [jnp.concatenate(ys[i:i + 2], axis=0).T for i in range(0, hpg, 2)]
    cz = cz_ref[0]
    o_ref[0] = (jnp.concatenate(tiles, axis=1) * (cz * _sigmoid(cz))).astype(o_ref.dtype)


def _nsa_mixer(u16, u32, kvcmp, overlap_t):
    b, s, _ = u16.shape
    ck = min(KCHUNK, s)
    gw = HEADS_PER_GROUP * HEAD_DIM
    bg, ncp = kvcmp.shape[1], kvcmp.shape[2]
    n_s = overlap_t.shape[0]
    vs_t = _kv_tiles(u16[:, :, _OFF16["c_vs"]:_OFF16["c_vs"] + LANES], ck)
    vw_t = _kv_tiles(u16[:, :, _OFF16["c_vw"]:_OFF16["c_vw"] + LANES], KTILE)
    vc_t = kvcmp[1][:, :, :HEAD_DIM].transpose(0, 2, 1)
    cval = jnp.concatenate([vc_t, jnp.broadcast_to(overlap_t[None], (bg, n_s, ncp))], axis=1)
    rows_c = HEAD_DIM + n_s

    def k_spec(name):
        base = _OFF16[name + "0"] // LANES
        return pl.BlockSpec((1, s, LANES), lambda bi, g, qi: (bi, 0, base + g))

    def vt_spec(tile):
        return pl.BlockSpec((1, 1, s // tile, HEAD_DIM, tile), lambda bi, g, qi: (bi, g, 0, 0, 0))

    return pl.pallas_call(
        functools.partial(_nsa_kernel, ck=ck, win=WIN, n_sel=min(SLC_N, n_s)),
        grid=(b, C_KV_GROUPS, s // QBLK),
        in_specs=[pl.BlockSpec((1, QBLK, gw), lambda bi, g, qi: (bi, qi, _OFF16["c_q"] // gw + g)),
                  pl.BlockSpec((1, 1, ncp, LANES), lambda bi, g, qi: (0, bi * C_KV_GROUPS + g, 0, 0)),
                  pl.BlockSpec((1, rows_c, ncp), lambda bi, g, qi: (bi * C_KV_GROUPS + g, 0, 0)),
                  k_spec("c_ks"), vt_spec(ck), k_spec("c_kw"), vt_spec(KTILE),
                  pl.BlockSpec((1, QBLK, LANES), lambda bi, g, qi: (bi, qi, _OFF32["c_g0"] // LANES + g)),
                  pl.BlockSpec((1, QBLK, gw), lambda bi, g, qi: (bi, qi, _OFF32["c_z"] // gw + g))],
        out_specs=pl.BlockSpec((1, QBLK, gw), lambda bi, g, qi: (bi, qi, g)),
        out_shape=jax.ShapeDtypeStruct((b, s, N_HEADS * HEAD_DIM), _MXU_DTYPE),
        scratch_shapes=[pltpu.VMEM((n_s + ck // SLC_LEN, QBLK), F32),
                        pltpu.VMEM((ck, HEADS_PER_GROUP * QBLK), F32),
                        pltpu.VMEM((ck, HEADS_PER_GROUP * QBLK), F32)],
        compiler_params=pltpu.CompilerParams(dimension_semantics=("arbitrary", "arbitrary", "arbitrary"),
                                             vmem_limit_bytes=VMEM_LIMIT),
        name="nsa_mixer",
    )(u16, kvcmp, cval, u16, vs_t, u16, vw_t, u32, u32)


def _dsa_kernel(q_ref, iq_ref, k_ref, v_ref, ik_ref, iw_ref, bz_ref, o_ref, keys_ref, bias_ref, cut_ref,
                sa_ref, sb_ref, *, ck, topk, nbits):
    qi = pl.program_id(1)
    q0 = qi * QBLK
    nch = q0 // ck + 1
    tq = q0 + lax.broadcasted_iota(I32, (1, QBLK), 1)
    krow = lax.broadcasted_iota(I32, (ck, 1), 0)
    srow = lax.broadcasted_iota(I32, (8, 1), 0)

    iq_all = _heads_t(iq_ref[0].astype(F32), IDX_HEADS * IDX_DIM // LANES, IDX_DIM)
    iw_t = (iw_ref[0] * (IDX_HEADS ** -0.5 * IDX_DIM ** -0.5)).T

    def score_body(c, _):
        off = pl.multiple_of(c * ck, ck)
        rel = _dot(ik_ref[0, pl.ds(off, ck), :], iq_all)
        sc = jnp.maximum(rel[:, 0:QBLK], 0.0) * iw_t[0:1]
        for h in range(1, IDX_HEADS):
            sc = sc + jnp.maximum(rel[:, h * QBLK:(h + 1) * QBLK], 0.0) * iw_t[h:h + 1]
        sc = jnp.where(off + krow <= tq, sc, NEG)
        sc = jnp.where(sc == 0.0, 0.0, sc)
        bits = pltpu.bitcast(sc, I32)
        keys_ref[c] = bits ^ (lax.shift_right_arithmetic(bits, 31) & 0x7FFFFFFF)
        return 0

    lax.fori_loop(0, nch, score_body, 0)

    n_acc = 4

    def count(pred):
        def body(c, accs):
            accs = list(accs)
            for r in range(ck // 8):
                k = keys_ref[c, r * 8:(r + 1) * 8, :]
                accs[r % n_acc] = accs[r % n_acc] + jnp.where(pred(k, c * ck + r * 8 + srow), 1.0, 0.0)
            return tuple(accs)
        accs = lax.fori_loop(0, nch, body, tuple(jnp.zeros((8, LANES), F32) for _ in range(n_acc)))
        return jnp.sum(sum(accs[1:], accs[0]), axis=0, keepdims=True)

    def bit_body(i, thr):
        cand = thr + lax.shift_left(jnp.int32(1), 31 - i)
        return jnp.where(count(lambda k, _: k >= cand) >= topk, cand, thr)

    thr = lax.fori_loop(0, 32, bit_body, jnp.full((1, LANES), -2 ** 31, I32))
    c_gt = count(lambda k, _: k > thr)
    c_eq = count(lambda k, _: k >= thr) - c_gt
    need = topk - c_gt

    cut_ref[...] = jnp.full(cut_ref.shape, 2 ** nbits, I32)

    @pl.when(jnp.max(jnp.where(c_eq > need, 1.0, 0.0)) > 0.0)
    def _():
        def tie_body(i, cut):
            cand = cut + lax.shift_left(jnp.int32(1), nbits - 1 - i)
            below = count(lambda k, kpos: (k == thr) & (kpos < cand))
            return jnp.where(below < need, cand, cut)
        cut = lax.fori_loop(0, nbits, tie_body, jnp.zeros((1, LANES), I32))
        cut_ref[...] = jnp.broadcast_to(cut, cut_ref.shape)

    cut = cut_ref[0:1, :]

    def bias_body(c, _):
        k = keys_ref[jnp.minimum(c, nch - 1)]
        kpos = c * ck + krow
        chosen = (k > thr) | ((k == thr) & (kpos <= cut))
        bias_ref[c] = jnp.where(chosen & (kpos <= tq), 0.0, NEG).astype(bias_ref.dtype)
        return 0

    lax.fori_loop(0, 2 * ((nch + 1) // 2), bias_body, 0)

    q_all = _with_mask_rows(_heads_t(q_ref[0].astype(F32), N_HEADS * HEAD_DIM // LANES, HEAD_DIM))

    out = _finish(_attend_chunks(k_ref.at[0], v_ref.at[0, 0], lambda c: bias_ref[c], q_all, nch, ck,
                                 _init_state(N_HEADS * QBLK, HEAD_DIM), sa_ref, sb_ref))
    tiles = [jnp.concatenate([out[:, h * QBLK:(h + 1) * QBLK], out[:, (h + 1) * QBLK:(h + 2) * QBLK]], axis=0).T
             for h in range(0, N_HEADS, 2)]
    bz = bz_ref[0]
    o_ref[0] = (jnp.concatenate(tiles, axis=1) * (bz * _sigmoid(bz))).astype(o_ref.dtype)


def _dsa_mixer(u16, u32):
    b, s, _ = u16.shape
    ck = min(KCHUNK, s)
    w = N_HEADS * HEAD_DIM
    iqw = IDX_HEADS * IDX_DIM
    v_t = _kv_tiles(u16[:, :, _OFF16["b_v"]:_OFF16["b_v"] + HEAD_DIM], ck)

    def k_spec(name):
        return pl.BlockSpec((1, s, LANES), lambda bi, qi: (bi, 0, _OFF16[name] // LANES))

    return pl.pallas_call(
        functools.partial(_dsa_kernel, ck=ck, topk=min(DSA_TOPK, s // 4), nbits=int(s).bit_length()),
        grid=(b, s // QBLK),
        in_specs=[pl.BlockSpec((1, QBLK, w), lambda bi, qi: (bi, qi, _OFF16["b_q"] // w)),
                  pl.BlockSpec((1, QBLK, iqw), lambda bi, qi: (bi, qi, _OFF16["i_q"] // iqw)),
                  k_spec("b_k"),
                  pl.BlockSpec((1, 1, s // ck, HEAD_DIM, ck), lambda bi, qi: (bi, 0, 0, 0, 0)),
                  k_spec("i_k"),
                  pl.BlockSpec((1, QBLK, LANES), lambda bi, qi: (bi, qi, _OFF32["i_w"] // LANES)),
                  pl.BlockSpec((1, QBLK, w), lambda bi, qi: (bi, qi, _OFF32["b_z"] // w))],
        out_specs=pl.BlockSpec((1, QBLK, w), lambda bi, qi: (bi, qi, 0)),
        out_shape=jax.ShapeDtypeStruct((b, s, w), _MXU_DTYPE),
        scratch_shapes=[pltpu.VMEM((s // ck, ck, QBLK), I32),
                        pltpu.VMEM((s // ck + s // ck % 2, ck, QBLK), _MXU_DTYPE),
                        pltpu.VMEM((8, LANES), I32),
                        pltpu.VMEM((ck, N_HEADS * QBLK), F32),
                        pltpu.VMEM((ck, N_HEADS * QBLK), F32)],
        compiler_params=pltpu.CompilerParams(dimension_semantics=("arbitrary", "arbitrary"),
                                             vmem_limit_bytes=VMEM_LIMIT),
        name="dsa_mixer",
    )(u16, u16, u16, v_t, u16, u32, u32)


def _merge_kernel(x_ref, ya_ref, yb_ref, yc_ref, g0_ref, g1_ref, g2_ref, wa_ref, wb_ref, wc_ref, wo_ref,
                  lg_ref, lb_ref, o32_ref, o16_ref, *, alpha):
    m = (_sigmoid(g0_ref[...]) * _dot(ya_ref[...], wa_ref[...])
         + _sigmoid(g1_ref[...]) * _dot(yb_ref[...], wb_ref[...])
         + _sigmoid(g2_ref[...]) * _dot(yc_ref[...], wc_ref[...]))
    z = alpha * x_ref[...] + _dot(m.astype(_MXU_DTYPE), wo_ref[...])
    mu = jnp.mean(z, axis=1, keepdims=True)
    zc = z - mu
    var = jnp.mean(zc * zc, axis=1, keepdims=True)
    y = zc * lax.rsqrt(var + LN_EPS) * lg_ref[...] + lb_ref[...]
    o32_ref[...] = y
    o16_ref[...] = y.astype(o16_ref.dtype)


def _merge(x, ya, yb, yc, u32, wa, wb, wc, wo, lg, lb, alpha, tm=512):
    m, d = x.shape
    w = ya.shape[1]
    gbase = _OFF32["g_merge"] // d
    row = lambda i: (i, 0)
    const = lambda i: (0, 0)
    return pl.pallas_call(
        functools.partial(_merge_kernel, alpha=alpha),
        grid=(m // tm,),
        in_specs=[pl.BlockSpec((tm, d), row),
                  pl.BlockSpec((tm, w), row), pl.BlockSpec((tm, w), row), pl.BlockSpec((tm, w), row),
                  pl.BlockSpec((tm, d), lambda i: (i, gbase)),
                  pl.BlockSpec((tm, d), lambda i: (i, gbase + 1)),
                  pl.BlockSpec((tm, d), lambda i: (i, gbase + 2)),
                  pl.BlockSpec((w, d), const), pl.BlockSpec((w, d), const), pl.BlockSpec((w, d), const),
                  pl.BlockSpec((d, d), const), pl.BlockSpec((1, d), const), pl.BlockSpec((1, d), const)],
        out_specs=[pl.BlockSpec((tm, d), row), pl.BlockSpec((tm, d), row)],
        out_shape=[jax.ShapeDtypeStruct((m, d), F32), jax.ShapeDtypeStruct((m, d), _MXU_DTYPE)],
        compiler_params=pltpu.CompilerParams(dimension_semantics=("arbitrary",), vmem_limit_bytes=VMEM_LIMIT),
        name="merge_out_ln",
    )(x, ya, yb, yc, u32, u32, u32, wa, wb, wc, wo, lg, lb)


def _overlap_matrix(ncp, s):
    n_c = (s - CMP_LEN) // CMP_STRIDE + 1
    c_start = np.arange(ncp) * CMP_STRIDE
    s_start = np.arange(s // SLC_LEN) * SLC_LEN
    ov = (c_start[:, None] <= s_start[None, :] + SLC_LEN - 1) & (c_start[:, None] + CMP_LEN - 1 >= s_start[None, :])
    ov &= (np.arange(ncp) < n_c)[:, None]
    return ov.astype(np.float32)


def _layer(x32, x16, p, consts, alpha):
    b, s, d = x32.shape
    m = b * s
    x16 = x16.reshape(m, d)
    u32 = _matmul_bias(x16, p["w32"], p["b32"], F32, min(1024, m), _TN32, "in_proj_f32").reshape(b, s, _N32)
    u16 = _matmul_bias(x16, p["w16"], p["b16"], _MXU_DTYPE, min(1024, m), _TN16, "in_proj_bf16").reshape(b, s, _N16)

    y_a = _pool_mixer(u32, p["pool_w"], p["pool_b"], p["pool_scale"])

    nch16 = s // CMP_STRIDE
    tok = u32[:, :, _OFF32["c_kc"]:_OFF32["c_kc"] + 2 * LANES]
    chunks = (tok.reshape(b, nch16, CMP_STRIDE, 2, C_KV_GROUPS, HEAD_DIM)
              .transpose(3, 0, 4, 1, 2, 5).reshape(2, b * C_KV_GROUPS, nch16, CMP_STRIDE * HEAD_DIM))
    kvcmp = _compress(chunks, p["cmp_pos"], p["cmp_w1"], p["cmp_w2"])

    y_c = _nsa_mixer(u16, u32, kvcmp, consts["overlap_t"])
    y_b = _dsa_mixer(u16, u32)

    x32n, x16n = _merge(x32.reshape(m, d), y_a.reshape(m, -1), y_b.reshape(m, -1), y_c.reshape(m, -1),
                        u32.reshape(m, _N32), p["w_pa"], p["w_pb"], p["w_pc"], p["w_o"], p["ln_g"], p["ln_b"], alpha)
    return x32n.reshape(b, s, d), x16n.reshape(b, s, d)


def _prepare_params(w_in, b_in, pool_w, pool_b, pool_scale, cmp_pos_k, cmp_pos_v, cmp_w1_k, cmp_w2_k,
                    cmp_w1_v, cmp_w2_v, w_proj_a, w_proj_b, w_proj_c, w_o, ln_g, ln_b):
    nl = w_in.shape[0]
    mx = _MXU_DTYPE
    half = CMP_LEN // 2
    pos = jnp.stack([cmp_pos_k, cmp_pos_v], axis=1).reshape(nl, 2, 2, half * HEAD_DIM)
    return {
        "w32": (_gather_cols(w_in, _IDX32) * _SCALE32).astype(mx),
        "b32": (_gather_cols(b_in, _IDX32) * _SCALE32)[:, None, :],
        "w16": (_gather_cols(w_in, _IDX16) * _SCALE16).astype(mx),
        "b16": (_gather_cols(b_in, _IDX16) * _SCALE16)[:, None, :],
        "pool_w": pool_w.astype(mx),
        "pool_b": pool_b.reshape(nl, 1, -1),
        "pool_scale": pool_scale.reshape(nl, 1, -1),
        "cmp_pos": pos,
        "cmp_w1": jnp.stack([cmp_w1_k, cmp_w1_v], axis=1).astype(mx),
        "cmp_w2": jnp.concatenate([jnp.stack([cmp_w2_k, cmp_w2_v], axis=1)] * 2, axis=-1).astype(mx),
        "w_pa": w_proj_a.astype(mx), "w_pb": w_proj_b.astype(mx), "w_pc": w_proj_c.astype(mx),
        "w_o": w_o.astype(mx),
        "ln_g": ln_g[:, None, :], "ln_b": ln_b[:, None, :],
    }


def kernel(x, w_in, b_in, pool_w, pool_b, pool_scale, cmp_pos_k, cmp_pos_v, cmp_w1_k, cmp_w2_k, cmp_w1_v, cmp_w2_v, w_proj_a, w_proj_b, w_proj_c, w_o, ln_g, ln_b):
    depth = w_in.shape[0]
    s = x.shape[1]
    params = _prepare_params(w_in, b_in, pool_w, pool_b, pool_scale, cmp_pos_k, cmp_pos_v, cmp_w1_k, cmp_w2_k,
                             cmp_w1_v, cmp_w2_v, w_proj_a, w_proj_b, w_proj_c, w_o, ln_g, ln_b)
    consts = {"overlap_t": jnp.asarray(_overlap_matrix(s // CMP_STRIDE, s).T, _MXU_DTYPE)}
    alpha = (2 * depth) ** 0.25
    h32, h16 = x, x.astype(_MXU_DTYPE)
    for l in range(depth):
        h32, h16 = _layer(h32, h16, {k: v[l] for k, v in params.items()}, consts, alpha)
    return h32
```

```python
import functools

import numpy as np
import jax
import jax.numpy as jnp
from jax import lax
from jax.experimental import pallas as pl
from jax.experimental.pallas import tpu as pltpu

F32 = jnp.float32
I32 = jnp.int32
_MXU_DTYPE = jnp.bfloat16

D_MODEL = 1024
HEAD_DIM = 64
LANES = 128
POOL_WINDOWS = (2, 4, 8, 16)
POOL_GC = 128
N_HEADS = 8
IDX_HEADS = 8
IDX_DIM = 32
DSA_TOPK = 256
C_KV_GROUPS = 2
HEADS_PER_GROUP = N_HEADS // C_KV_GROUPS
CMP_LEN = 32
CMP_STRIDE = 16
SLC_LEN = 64
SLC_N = 16
WIN = 512
FORCE_BONUS = 1e4
LN_EPS = 1e-5
NEG = -1e30
QBLK = 128
KCHUNK = 512
VMEM_LIMIT = 56 * 1024 * 1024

_IN_WIDTHS = (512, 512, 512, 64, 64, 512, 256, 32, 8, 512, 128, 128, 128, 128, 128, 128, 24, 512, 3072)
_IN_NAMES = ("a_x", "a_z", "b_q", "b_k", "b_v", "b_z", "i_q", "i_k", "i_w", "c_q", "c_kc", "c_vc",
             "c_ks", "c_vs", "c_kw", "c_vw", "c_g", "c_z", "g_merge")
_N_IN = sum(_IN_WIDTHS)
_OFF = dict(zip(_IN_NAMES, np.cumsum((0,) + _IN_WIDTHS[:-1])))
_WID = dict(zip(_IN_NAMES, _IN_WIDTHS))


def _seg(name, lo=0, hi=None):
    hi = _WID[name] if hi is None else hi
    return np.arange(_OFF[name] + lo, _OFF[name] + hi)


def _pad(n):
    return np.full((n,), _N_IN)


def _layout32():
    segs, off, pos = [], {}, 0

    def add(name, idx):
        nonlocal pos
        off[name] = pos
        segs.append(idx)
        pos += len(idx)

    add("a_x", _seg("a_x"))
    add("a_z", _seg("a_z"))
    add("b_z", _seg("b_z"))
    add("c_z", _seg("c_z"))
    add("g_merge", _seg("g_merge"))
    add("c_kc", _seg("c_kc"))
    add("c_vc", _seg("c_vc"))
    add("i_w", np.concatenate([_seg("i_w"), _pad(LANES - 8)]))
    for g in range(C_KV_GROUPS):
        add(f"c_g{g}", np.concatenate([_seg("c_g", 12 * g, 12 * g + 12), _pad(LANES - 12)]))
    idx = np.concatenate(segs)
    return idx, np.ones((len(idx),), np.float32), off


def _layout16():
    segs, scales, off, pos = [], [], {}, 0

    def add(name, idx, scale=1.0):
        nonlocal pos
        off[name] = pos
        segs.append(idx)
        scales.append(np.full((len(idx),), scale, np.float32))
        pos += len(idx)

    qk_scale = HEAD_DIM ** -0.5 * float(np.log2(np.e))
    add("b_q", _seg("b_q"), qk_scale)
    add("c_q", _seg("c_q"), qk_scale)
    add("i_q", _seg("i_q"))
    for name in ("c_ks", "c_kw"):
        for g in range(C_KV_GROUPS):
            one = _seg(name, HEAD_DIM * g, HEAD_DIM * (g + 1))
            add(f"{name}{g}", np.concatenate([one, one]))
    add("b_k", np.concatenate([_seg("b_k")] * 2))
    add("i_k", np.concatenate([_seg("i_k")] * (LANES // IDX_DIM)))
    add("c_vs", _seg("c_vs"))
    add("c_vw", _seg("c_vw"))
    add("b_v", np.concatenate([_seg("b_v"), _pad(LANES - HEAD_DIM)]))
    add("pad", _pad(LANES))
    return np.concatenate(segs), np.concatenate(scales), off


def _gather_cols(a, idx):
    pieces, i = [], 0
    while i < len(idx):
        j = i + 1
        if idx[i] == _N_IN:
            while j < len(idx) and idx[j] == _N_IN:
                j += 1
            pieces.append(jnp.zeros(a.shape[:-1] + (j - i,), a.dtype))
        else:
            while j < len(idx) and idx[j] == idx[j - 1] + 1:
                j += 1
            pieces.append(a[..., int(idx[i]):int(idx[i]) + (j - i)])
        i = j
    return jnp.concatenate(pieces, axis=-1)


_IDX32, _SCALE32, _OFF32 = _layout32()
_IDX16, _SCALE16, _OFF16 = _layout16()
_N32 = len(_IDX32)
_N16 = len(_IDX16)
_TN32 = 640
_TN16 = 512
assert _N32 % _TN32 == 0 and _N16 % _TN16 == 0


def _sigmoid(x):
    return 1.0 / (1.0 + jnp.exp(-x))


def _dot(a, b):
    return jnp.dot(a, b, preferred_element_type=F32)


def _mm_bias_kernel(x_ref, w_ref, b_ref, o_ref):
    o_ref[...] = (_dot(x_ref[...], w_ref[...]) + b_ref[...]).astype(o_ref.dtype)


def _matmul_bias(x, w, b, out_dtype, tm, tn, name):
    m, k = x.shape
    n = w.shape[1]
    return pl.pallas_call(
        _mm_bias_kernel,
        grid=(m // tm, n // tn),
        in_specs=[pl.BlockSpec((tm, k), lambda i, j: (i, 0)),
                  pl.BlockSpec((k, tn), lambda i, j: (0, j)),
                  pl.BlockSpec((1, tn), lambda i, j: (0, j))],
        out_specs=pl.BlockSpec((tm, tn), lambda i, j: (i, j)),
        out_shape=jax.ShapeDtypeStruct((m, n), out_dtype),
        compiler_params=pltpu.CompilerParams(dimension_semantics=("arbitrary", "arbitrary"),
                                             vmem_limit_bytes=VMEM_LIMIT),
        name=name,
    )(x, w, b)


_HALO = 16


def _pool_kernel(xa_ref, halo_ref, az_ref, pw_ref, pb_ref, ps_ref, o_ref, *, tb):
    i = pl.program_id(1)
    cur = xa_ref[0]
    halo = jnp.where(i > 0, halo_ref[0], 0.0)
    ext = jnp.concatenate([halo, cur], axis=0)
    pos = (i * tb + 1 + lax.broadcasted_iota(I32, (tb, 1), 0)).astype(F32)
    outs = []
    for g, wnd in enumerate(POOL_WINDOWS):
        s = ext[:, g * POOL_GC:(g + 1) * POOL_GC]
        k = 1
        while k < wnd:
            s = s + pltpu.roll(s, k, axis=0)
            k *= 2
        mean = s[_HALO:] / jnp.minimum(pos, float(wnd))
        pooled = mean - cur[:, g * POOL_GC:(g + 1) * POOL_GC]
        outs.append(_dot(pooled.astype(_MXU_DTYPE), pw_ref[g]))
    y = jnp.concatenate(outs, axis=1) + pb_ref[...]
    az = az_ref[0]
    o_ref[0] = (y * ps_ref[...] * (az * _sigmoid(az))).astype(o_ref.dtype)


def _pool_mixer(u32, pool_w, pool_b, pool_scale, tb=512):
    b, s, _ = u32.shape
    hb = tb // _HALO
    return pl.pallas_call(
        functools.partial(_pool_kernel, tb=tb),
        grid=(b, s // tb),
        in_specs=[pl.BlockSpec((1, tb, 512), lambda bi, i: (bi, i, _OFF32["a_x"] // 512)),
                  pl.BlockSpec((1, _HALO, 512), lambda bi, i: (bi, jnp.maximum(i * hb - 1, 0), _OFF32["a_x"] // 512)),
                  pl.BlockSpec((1, tb, 512), lambda bi, i: (bi, i, _OFF32["a_z"] // 512)),
                  pl.BlockSpec((4, POOL_GC, POOL_GC), lambda bi, i: (0, 0, 0)),
                  pl.BlockSpec((1, 512), lambda bi, i: (0, 0)),
                  pl.BlockSpec((1, 512), lambda bi, i: (0, 0))],
        out_specs=pl.BlockSpec((1, tb, 512), lambda bi, i: (bi, i, 0)),
        out_shape=jax.ShapeDtypeStruct((b, s, 512), _MXU_DTYPE),
        compiler_params=pltpu.CompilerParams(dimension_semantics=("arbitrary", "arbitrary"),
                                             vmem_limit_bytes=VMEM_LIMIT),
        name="pool_mixer",
    )(u32, u32, u32, pool_w, pool_b, pool_scale)


def _compress_kernel(ch_ref, pos_ref, w1_ref, w2_ref, o_ref):
    ch = ch_ref[0, 0]
    pos = pos_ref[0]
    w1 = w1_ref[0]
    half = ch.shape[1]
    n = ch.shape[0]
    a = _dot((ch + pos[0:1]).astype(_MXU_DTYPE), w1[:half])
    bb = _dot((ch + pos[1:2]).astype(_MXU_DTYPE), w1[half:])
    h = a + pltpu.roll(bb, n - 1, axis=0)
    h = h * _sigmoid(h)
    o_ref[0, 0] = _dot(h.astype(_MXU_DTYPE), w2_ref[0]).astype(o_ref.dtype)


def _compress(chunks, pos, w1, w2dup):
    _, bg, n, width = chunks.shape
    return pl.pallas_call(
        _compress_kernel,
        grid=(2, bg),
        in_specs=[pl.BlockSpec((1, 1, n, width), lambda kv, i: (kv, i, 0, 0)),
                  pl.BlockSpec((1, 2, width), lambda kv, i: (kv, 0, 0)),
                  pl.BlockSpec((1, 2 * width, HEAD_DIM), lambda kv, i: (kv, 0, 0)),
                  pl.BlockSpec((1, HEAD_DIM, LANES), lambda kv, i: (kv, 0, 0))],
        out_specs=pl.BlockSpec((1, 1, n, LANES), lambda kv, i: (kv, i, 0, 0)),
        out_shape=jax.ShapeDtypeStruct((2, bg, n, LANES), _MXU_DTYPE),
        compiler_params=pltpu.CompilerParams(dimension_semantics=("arbitrary", "arbitrary"),
                                             vmem_limit_bytes=VMEM_LIMIT),
        name="nsa_compress",
    )(chunks, pos, w1, w2dup)


KTILE = 128


def _heads_t(qf, n_tiles, width):
    lane = lax.broadcasted_iota(I32, (1, LANES), 1)
    per_tile = LANES // width
    out = []
    for tix in range(n_tiles):
        qt = qf[:, tix * LANES:(tix + 1) * LANES]
        for j in range(per_tile):
            keep = (lane >= j * width) & (lane < (j + 1) * width)
            out.append(jnp.where(keep, qt, 0.0).T.astype(_MXU_DTYPE))
    return jnp.concatenate(out, axis=1)


def _init_state(cols, rows):
    return (jnp.full((1, cols), -1e38, F32), jnp.zeros((1, cols), F32), jnp.zeros((rows, cols), F32))


def _with_mask_rows(q_all):
    cols = q_all.shape[1]
    r = lax.broadcasted_iota(I32, (LANES, cols), 0)
    c = lax.broadcasted_iota(I32, (LANES, cols), 1)
    eye = jnp.where((c & (LANES - 1)) == r, 1.0, 0.0).astype(_MXU_DTYPE)
    return jnp.concatenate([q_all, eye], axis=0)


def _update(s, vt_blk, state):
    m, l, acc = state
    m_new = jnp.maximum(m, jnp.max(s, axis=0, keepdims=True))
    a = jnp.exp2(m - m_new)
    p = jnp.exp2(s - m_new)
    l = a * l + jnp.sum(p, axis=0, keepdims=True)
    acc = a * acc + _dot(vt_blk, p.astype(_MXU_DTYPE))
    return m_new, l, acc


def _scores(k_blk, bias, q_aug):
    return _dot(jnp.concatenate([k_blk, bias], axis=1), q_aug)


def _attend(k_blk, vt_blk, q_aug, bias, state):
    return _update(_scores(k_blk, bias, q_aug), vt_blk, state)


def _attend_chunks(k_ref, vt_ref, bias_of, q_aug, nch, ck, state, s_a, s_b):
    last = nch - 1
    trips = (nch + 1) // 2

    def fill(dst, c):
        cc = jnp.minimum(c, last)
        cb = jnp.minimum(c, 2 * trips - 1)
        dst[...] = _scores(k_ref[pl.ds(pl.multiple_of(cc * ck, ck), ck), :], bias_of(cb), q_aug)

    def body(j, state):
        c = 2 * j
        fill(s_b, c + 1)
        state = _update(s_a[...], vt_ref[c], state)
        fill(s_a, c + 2)
        return _update(s_b[...], vt_ref[jnp.minimum(c + 1, last)], state)

    fill(s_a, 0)
    return lax.fori_loop(0, trips, body, state)


def _finish(state):
    _, l, acc = state
    return acc / l


def _kv_tiles(v, tile):
    b, s, w = v.shape
    c = w // HEAD_DIM
    return v.reshape(b, s // tile, tile, c, HEAD_DIM).transpose(0, 3, 1, 4, 2)


def _nsa_kernel(q_ref, kc_ref, cv_ref, ks_ref, vs_ref, kw_ref, vw_ref, cg_ref, cz_ref, o_ref, selb_ref,
                sa_ref, sb_ref, *, ck, win, n_sel):
    qi = pl.program_id(2)
    q0 = qi * QBLK
    hpg = HEADS_PER_GROUP
    cols = hpg * QBLK
    tq = q0 + lax.broadcasted_iota(I32, (1, QBLK), 1)
    q_all = _with_mask_rows(_heads_t(q_ref[0].astype(F32), hpg // 2, HEAD_DIM))

    ncp = kc_ref.shape[2]
    n_s = cv_ref.shape[1] - HEAD_DIM
    cend = lax.broadcasted_iota(I32, (ncp, 1), 0) * CMP_STRIDE + (CMP_LEN - 1)
    out = _finish(_attend(kc_ref[0, 0], cv_ref[0], q_all, jnp.where(cend <= tq, 0.0, NEG).astype(_MXU_DTYPE),
                          _init_state(cols, HEAD_DIM + n_s)))
    seen = jnp.concatenate([tq >= CMP_LEN - 1] * hpg, axis=1)
    out = jnp.where(seen, out, 0.0)
    o_c = out[:HEAD_DIM]
    work = out[HEAD_DIM:, 0:QBLK]
    for r in range(1, hpg):
        work = work + out[HEAD_DIM:, r * QBLK:(r + 1) * QBLK]

    ji = lax.broadcasted_iota(I32, (n_s, 1), 0)
    jf = ji.astype(F32)
    blk = lax.shift_right_logical(tq, 6)
    forced = (ji == 0) | (ji == blk) | (ji == blk - 1)
    work = work + jnp.where(forced, FORCE_BONUS, 0.0)
    work = jnp.where(ji * SLC_LEN <= tq, work, NEG)
    sel = jnp.zeros((n_s, QBLK), F32)
    for _ in range(n_sel):
        mx = jnp.max(work, axis=0, keepdims=True)
        first = jnp.min(jnp.where(work == mx, jf, float(n_s)), axis=0, keepdims=True)
        hit = jf == first
        sel = jnp.where(hit, 1.0, sel)
        work = jnp.where(hit, -3e38, work)
    selb_ref[0:n_s, :] = jnp.where(sel > 0.5, 0.0, NEG)
    selb_ref[n_s:, :] = jnp.full((selb_ref.shape[0] - n_s, QBLK), NEG, F32)

    per_chunk = ck // SLC_LEN
    krow = lax.broadcasted_iota(I32, (ck, 1), 0)

    def sel_bias(c):
        rows = [jnp.broadcast_to(selb_ref[pl.ds(c * per_chunk + i, 1), :], (SLC_LEN, QBLK))
                for i in range(per_chunk)]
        return jnp.where(c * ck + krow <= tq, jnp.concatenate(rows, axis=0), NEG).astype(_MXU_DTYPE)

    o_s = _finish(_attend_chunks(ks_ref.at[0], vs_ref.at[0, 0], sel_bias, q_all, q0 // ck + 1, ck,
                                 _init_state(cols, HEAD_DIM), sa_ref, sb_ref))

    span = win + QBLK
    start = pl.multiple_of(jnp.maximum(q0 - win, 0), QBLK)
    kpos = start + lax.broadcasted_iota(I32, (span, 1), 0)
    bias = jnp.where((kpos <= tq) & (kpos > tq - win), 0.0, NEG).astype(_MXU_DTYPE)
    vw = jnp.concatenate([vw_ref[0, 0, start // KTILE + i] for i in range(span // KTILE)], axis=1)
    o_w = _finish(_attend(kw_ref[0, pl.ds(start, span), :], vw, q_all, bias, _init_state(cols, HEAD_DIM)))

    gate = _sigmoid(cg_ref[0].T[0:16])
    ys = []
    for r in range(hpg):
        c_ = slice(r * QBLK, (r + 1) * QBLK)
        ys.append(gate[3 * r:3 * r + 1] * o_c[:, c_] + gate[3 * r + 1:3 * r + 2] * o_s[:, c_]
                  + gate[3 * r + 2:3 * r + 3] * o_w[:, c_])
    tiles = [jnp.concatenate(ys[i:i + 2], axis=0).T for i in range(0, hpg, 2)]
    cz = cz_ref[0]
    o_ref[0] = (jnp.concatenate(tiles, axis=1) * (cz * _sigmoid(cz))).astype(o_ref.dtype)


def _nsa_mixer(u16, u32, kvcmp, overlap_t):
    b, s, _ = u16.shape
    ck = min(KCHUNK, s)
    gw = HEADS_PER_GROUP * HEAD_DIM
    bg, ncp = kvcmp.shape[1], kvcmp.shape[2]
    n_s = overlap_t.shape[0]
    vs_t = _kv_tiles(u16[:, :, _OFF16["c_vs"]:_OFF16["c_vs"] + LANES], ck)
    vw_t = _kv_tiles(u16[:, :, _OFF16["c_vw"]:_OFF16["c_vw"] + LANES], KTILE)
    vc_t = kvcmp[1][:, :, :HEAD_DIM].transpose(0, 2, 1)
    cval = jnp.concatenate([vc_t, jnp.broadcast_to(overlap_t[None], (bg, n_s, ncp))], axis=1)
    rows_c = HEAD_DIM + n_s

    def k_spec(name):
        base = _OFF16[name + "0"] // LANES
        return pl.BlockSpec((1, s, LANES), lambda bi, g, qi: (bi, 0, base + g))

    def vt_spec(tile):
        return pl.BlockSpec((1, 1, s // tile, HEAD_DIM, tile), lambda bi, g, qi: (bi, g, 0, 0, 0))

    return pl.pallas_call(
        functools.partial(_nsa_kernel, ck=ck, win=WIN, n_sel=min(SLC_N, n_s)),
        grid=(b, C_KV_GROUPS, s // QBLK),
        in_specs=[pl.BlockSpec((1, QBLK, gw), lambda bi, g, qi: (bi, qi, _OFF16["c_q"] // gw + g)),
                  pl.BlockSpec((1, 1, ncp, LANES), lambda bi, g, qi: (0, bi * C_KV_GROUPS + g, 0, 0)),
                  pl.BlockSpec((1, rows_c, ncp), lambda bi, g, qi: (bi * C_KV_GROUPS + g, 0, 0)),
                  k_spec("c_ks"), vt_spec(ck), k_spec("c_kw"), vt_spec(KTILE),
                  pl.BlockSpec((1, QBLK, LANES), lambda bi, g, qi: (bi, qi, _OFF32["c_g0"] // LANES + g)),
                  pl.BlockSpec((1, QBLK, gw), lambda bi, g, qi: (bi, qi, _OFF32["c_z"] // gw + g))],
        out_specs=pl.BlockSpec((1, QBLK, gw), lambda bi, g, qi: (bi, qi, g)),
        out_shape=jax.ShapeDtypeStruct((b, s, N_HEADS * HEAD_DIM), _MXU_DTYPE),
        scratch_shapes=[pltpu.VMEM((n_s + ck // SLC_LEN, QBLK), F32),
                        pltpu.VMEM((ck, HEADS_PER_GROUP * QBLK), F32),
                        pltpu.VMEM((ck, HEADS_PER_GROUP * QBLK), F32)],
        compiler_params=pltpu.CompilerParams(dimension_semantics=("arbitrary", "arbitrary", "arbitrary"),
                                             vmem_limit_bytes=VMEM_LIMIT),
        name="nsa_mixer",
    )(u16, kvcmp, cval, u16, vs_t, u16, vw_t, u32, u32)


def _dsa_kernel(q_ref, iq_ref, k_ref, v_ref, ik_ref, iw_ref, bz_ref, o_ref, keys_ref, bias_ref, cut_ref,
                sa_ref, sb_ref, *, ck, topk, nbits):
    qi = pl.program_id(1)
    q0 = qi * QBLK
    nch = q0 // ck + 1
    tq = q0 + lax.broadcasted_iota(I32, (1, QBLK), 1)
    krow = lax.broadcasted_iota(I32, (ck, 1), 0)
    srow = lax.broadcasted_iota(I32, (8, 1), 0)

    iq_all = _heads_t(iq_ref[0].astype(F32), IDX_HEADS * IDX_DIM // LANES, IDX_DIM)
    iw_t = (iw_ref[0] * (IDX_HEADS ** -0.5 * IDX_DIM ** -0.5)).T

    def fill_rel(dst, c):
        cc = jnp.minimum(c, nch - 1)
        dst[...] = _dot(ik_ref[0, pl.ds(pl.multiple_of(cc * ck, ck), ck), :], iq_all)

    def keys_from(src, c):
        rel = src[...]
        sc = jnp.maximum(rel[:, 0:QBLK], 0.0) * iw_t[0:1]
        for h in range(1, IDX_HEADS):
            sc = sc + jnp.maximum(rel[:, h * QBLK:(h + 1) * QBLK], 0.0) * iw_t[h:h + 1]
        sc = jnp.where(c * ck + krow <= tq, sc, NEG)
        sc = jnp.where(sc == 0.0, 0.0, sc)
        bits = pltpu.bitcast(sc, I32)
        keys_ref[c] = bits ^ (lax.shift_right_arithmetic(bits, 31) & 0x7FFFFFFF)

    def score_body(j, _):
        c = 2 * j
        fill_rel(sb_ref, c + 1)
        keys_from(sa_ref, c)
        fill_rel(sa_ref, c + 2)
        keys_from(sb_ref, c + 1)
        return 0

    fill_rel(sa_ref, 0)
    lax.fori_loop(0, (nch + 1) // 2, score_body, 0)

    n_acc = 4

    def count(pred):
        def body(c, accs):
            accs = list(accs)
            for r in range(ck // 8):
                k = keys_ref[c, r * 8:(r + 1) * 8, :]
                accs[r % n_acc] = accs[r % n_acc] + jnp.where(pred(k, c * ck + r * 8 + srow), 1.0, 0.0)
            return tuple(accs)
        accs = lax.fori_loop(0, nch, body, tuple(jnp.zeros((8, LANES), F32) for _ in range(n_acc)))
        return jnp.sum(sum(accs[1:], accs[0]), axis=0, keepdims=True)

    def bit_body(i, thr):
        cand = thr + lax.shift_left(jnp.int32(1), 31 - i)
        return jnp.where(count(lambda k, _: k >= cand) >= topk, cand, thr)

    thr = lax.fori_loop(0, 32, bit_body, jnp.full((1, LANES), -2 ** 31, I32))
    c_gt = count(lambda k, _: k > thr)
    c_eq = count(lambda k, _: k >= thr) - c_gt
    need = topk - c_gt

    cut_ref[...] = jnp.full(cut_ref.shape, 2 ** nbits, I32)

    @pl.when(jnp.max(jnp.where(c_eq > need, 1.0, 0.0)) > 0.0)
    def _():
        def tie_body(i, cut):
            cand = cut + lax.shift_left(jnp.int32(1), nbits - 1 - i)
            below = count(lambda k, kpos: (k == thr) & (kpos < cand))
            return jnp.where(below < need, cand, cut)
        cut = lax.fori_loop(0, nbits, tie_body, jnp.zeros((1, LANES), I32))
        cut_ref[...] = jnp.broadcast_to(cut, cut_ref.shape)

    cut = cut_ref[0:1, :]

    def bias_body(c, _):
        k = keys_ref[jnp.minimum(c, nch - 1)]
        kpos = c * ck + krow
        chosen = (k > thr) | ((k == thr) & (kpos <= cut))
        bias_ref[c] = jnp.where(chosen & (kpos <= tq), 0.0, NEG).astype(bias_ref.dtype)
        return 0

    lax.fori_loop(0, 2 * ((nch + 1) // 2), bias_body, 0)

    q_all = _with_mask_rows(_heads_t(q_ref[0].astype(F32), N_HEADS * HEAD_DIM // LANES, HEAD_DIM))

    out = _finish(_attend_chunks(k_ref.at[0], v_ref.at[0, 0], lambda c: bias_ref[c], q_all, nch, ck,
                                 _init_state(N_HEADS * QBLK, HEAD_DIM), sa_ref, sb_ref))
    tiles = [jnp.concatenate([out[:, h * QBLK:(h + 1) * QBLK], out[:, (h + 1) * QBLK:(h + 2) * QBLK]], axis=0).T
             for h in range(0, N_HEADS, 2)]
    bz = bz_ref[0]
    o_ref[0] = (jnp.concatenate(tiles, axis=1) * (bz * _sigmoid(bz))).astype(o_ref.dtype)


def _dsa_mixer(u16, u32):
    b, s, _ = u16.shape
    ck = min(KCHUNK, s)
    w = N_HEADS * HEAD_DIM
    iqw = IDX_HEADS * IDX_DIM
    v_t = _kv_tiles(u16[:, :, _OFF16["b_v"]:_OFF16["b_v"] + HEAD_DIM], ck)

    def k_spec(name):
        return pl.BlockSpec((1, s, LANES), lambda bi, qi: (bi, 0, _OFF16[name] // LANES))

    return pl.pallas_call(
        functools.partial(_dsa_kernel, ck=ck, topk=min(DSA_TOPK, s // 4), nbits=int(s).bit_length()),
        grid=(b, s // QBLK),
        in_specs=[pl.BlockSpec((1, QBLK, w), lambda bi, qi: (bi, qi, _OFF16["b_q"] // w)),
                  pl.BlockSpec((1, QBLK, iqw), lambda bi, qi: (bi, qi, _OFF16["i_q"] // iqw)),
                  k_spec("b_k"),
                  pl.BlockSpec((1, 1, s // ck, HEAD_DIM, ck), lambda bi, qi: (bi, 0, 0, 0, 0)),
                  k_spec("i_k"),
                  pl.BlockSpec((1, QBLK, LANES), lambda bi, qi: (bi, qi, _OFF32["i_w"] // LANES)),
                  pl.BlockSpec((1, QBLK, w), lambda bi, qi: (bi, qi, _OFF32["b_z"] // w))],
        out_specs=pl.BlockSpec((1, QBLK, w), lambda bi, qi: (bi, qi, 0)),
        out_shape=jax.ShapeDtypeStruct((b, s, w), _MXU_DTYPE),
        scratch_shapes=[pltpu.VMEM((s // ck + s // ck % 2, ck, QBLK), I32),
                        pltpu.VMEM((s // ck + s // ck % 2, ck, QBLK), _MXU_DTYPE),
                        pltpu.VMEM((8, LANES), I32),
                        pltpu.VMEM((ck, N_HEADS * QBLK), F32),
                        pltpu.VMEM((ck, N_HEADS * QBLK), F32)],
        compiler_params=pltpu.CompilerParams(dimension_semantics=("arbitrary", "arbitrary"),
                                             vmem_limit_bytes=VMEM_LIMIT),
        name="dsa_mixer",
    )(u16, u16, u16, v_t, u16, u32, u32)


def _merge_kernel(x_ref, ya_ref, yb_ref, yc_ref, g0_ref, g1_ref, g2_ref, wa_ref, wb_ref, wc_ref, wo_ref,
                  lg_ref, lb_ref, o32_ref, o16_ref, *, alpha):
    m = (_sigmoid(g0_ref[...]) * _dot(ya_ref[...], wa_ref[...])
         + _sigmoid(g1_ref[...]) * _dot(yb_ref[...], wb_ref[...])
         + _sigmoid(g2_ref[...]) * _dot(yc_ref[...], wc_ref[...]))
    z = alpha * x_ref[...] + _dot(m.astype(_MXU_DTYPE), wo_ref[...])
    mu = jnp.mean(z, axis=1, keepdims=True)
    zc = z - mu
    var = jnp.mean(zc * zc, axis=1, keepdims=True)
    y = zc * lax.rsqrt(var + LN_EPS) * lg_ref[...] + lb_ref[...]
    o32_ref[...] = y
    o16_ref[...] = y.astype(o16_ref.dtype)


def _merge(x, ya, yb, yc, u32, wa, wb, wc, wo, lg, lb, alpha, tm=512):
    m, d = x.shape
    w = ya.shape[1]
    gbase = _OFF32["g_merge"] // d
    row = lambda i: (i, 0)
    const = lambda i: (0, 0)
    return pl.pallas_call(
        functools.partial(_merge_kernel, alpha=alpha),
        grid=(m // tm,),
        in_specs=[pl.BlockSpec((tm, d), row),
                  pl.BlockSpec((tm, w), row), pl.BlockSpec((tm, w), row), pl.BlockSpec((tm, w), row),
                  pl.BlockSpec((tm, d), lambda i: (i, gbase)),
                  pl.BlockSpec((tm, d), lambda i: (i, gbase + 1)),
                  pl.BlockSpec((tm, d), lambda i: (i, gbase + 2)),
                  pl.BlockSpec((w, d), const), pl.BlockSpec((w, d), const), pl.BlockSpec((w, d), const),
                  pl.BlockSpec((d, d), const), pl.BlockSpec((1, d), const), pl.BlockSpec((1, d), const)],
        out_specs=[pl.BlockSpec((tm, d), row), pl.BlockSpec((tm, d), row)],
        out_shape=[jax.ShapeDtypeStruct((m, d), F32), jax.ShapeDtypeStruct((m, d), _MXU_DTYPE)],
        compiler_params=pltpu.CompilerParams(dimension_semantics=("arbitrary",), vmem_limit_bytes=VMEM_LIMIT),
        name="merge_out_ln",
    )(x, ya, yb, yc, u32, u32, u32, wa, wb, wc, wo, lg, lb)


def _overlap_matrix(ncp, s):
    n_c = (s - CMP_LEN) // CMP_STRIDE + 1
    c_start = np.arange(ncp) * CMP_STRIDE
    s_start = np.arange(s // SLC_LEN) * SLC_LEN
    ov = (c_start[:, None] <= s_start[None, :] + SLC_LEN - 1) & (c_start[:, None] + CMP_LEN - 1 >= s_start[None, :])
    ov &= (np.arange(ncp) < n_c)[:, None]
    return ov.astype(np.float32)


def _layer(x32, x16, p, consts, alpha):
    b, s, d = x32.shape
    m = b * s
    x16 = x16.reshape(m, d)
    u32 = _matmul_bias(x16, p["w32"], p["b32"], F32, min(2048, m), _TN32, "in_proj_f32").reshape(b, s, _N32)
    u16 = _matmul_bias(x16, p["w16"], p["b16"], _MXU_DTYPE, min(2048, m), _TN16, "in_proj_bf16").reshape(b, s, _N16)

    y_a = _pool_mixer(u32, p["pool_w"], p["pool_b"], p["pool_scale"])

    nch16 = s // CMP_STRIDE
    tok = u32[:, :, _OFF32["c_kc"]:_OFF32["c_kc"] + 2 * LANES]
    chunks = (tok.reshape(b, nch16, CMP_STRIDE, 2, C_KV_GROUPS, HEAD_DIM)
              .transpose(3, 0, 4, 1, 2, 5).reshape(2, b * C_KV_GROUPS, nch16, CMP_STRIDE * HEAD_DIM))
    kvcmp = _compress(chunks, p["cmp_pos"], p["cmp_w1"], p["cmp_w2"])

    y_c = _nsa_mixer(u16, u32, kvcmp, consts["overlap_t"])
    y_b = _dsa_mixer(u16, u32)

    x32n, x16n = _merge(x32.reshape(m, d), y_a.reshape(m, -1), y_b.reshape(m, -1), y_c.reshape(m, -1),
                        u32.reshape(m, _N32), p["w_pa"], p["w_pb"], p["w_pc"], p["w_o"], p["ln_g"], p["ln_b"], alpha)
    return x32n.reshape(b, s, d), x16n.reshape(b, s, d)


def _prepare_params(w_in, b_in, pool_w, pool_b, pool_scale, cmp_pos_k, cmp_pos_v, cmp_w1_k, cmp_w2_k,
                    cmp_w1_v, cmp_w2_v, w_proj_a, w_proj_b, w_proj_c, w_o, ln_g, ln_b):
    nl = w_in.shape[0]
    mx = _MXU_DTYPE
    half = CMP_LEN // 2
    pos = jnp.stack([cmp_pos_k, cmp_pos_v], axis=1).reshape(nl, 2, 2, half * HEAD_DIM)
    return {
        "w32": (_gather_cols(w_in, _IDX32) * _SCALE32).astype(mx),
        "b32": (_gather_cols(b_in, _IDX32) * _SCALE32)[:, None, :],
        "w16": (_gather_cols(w_in, _IDX16) * _SCALE16).astype(mx),
        "b16": (_gather_cols(b_in, _IDX16) * _SCALE16)[:, None, :],
        "pool_w": pool_w.astype(mx),
        "pool_b": pool_b.reshape(nl, 1, -1),
        "pool_scale": pool_scale.reshape(nl, 1, -1),
        "cmp_pos": pos,
        "cmp_w1": jnp.stack([cmp_w1_k, cmp_w1_v], axis=1).astype(mx),
        "cmp_w2": jnp.concatenate([jnp.stack([cmp_w2_k, cmp_w2_v], axis=1)] * 2, axis=-1).astype(mx),
        "w_pa": w_proj_a.astype(mx), "w_pb": w_proj_b.astype(mx), "w_pc": w_proj_c.astype(mx),
        "w_o": w_o.astype(mx),
        "ln_g": ln_g[:, None, :], "ln_b": ln_b[:, None, :],
    }


def kernel(x, w_in, b_in, pool_w, pool_b, pool_scale, cmp_pos_k, cmp_pos_v, cmp_w1_k, cmp_w2_k, cmp_w1_v, cmp_w2_v, w_proj_a, w_proj_b, w_proj_c, w_o, ln_g, ln_b):
    depth = w_in.shape[0]
    s = x.shape[1]
    params = _prepare_params(w_in, b_in, pool_w, pool_b, pool_scale, cmp_pos_k, cmp_pos_v, cmp_w1_k, cmp_w2_k,
                             cmp_w1_v, cmp_w2_v, w_proj_a, w_proj_b, w_proj_c, w_o, ln_g, ln_b)
    consts = {"overlap_t": jnp.asarray(_overlap_matrix(s // CMP_STRIDE, s).T, _MXU_DTYPE)}
    alpha = (2 * depth) ** 0.25
    h32, h16 = x, x.astype(_MXU_DTYPE)
    for l in range(depth):
        h32, h16 = _layer(h32, h16, {k: v[l] for k, v in params.items()}, consts, alpha)
    return h32
```

```python
import functools

import numpy as np
import jax
import jax.numpy as jnp
from jax import lax
from jax.experimental import pallas as pl
from jax.experimental.pallas import tpu as pltpu

F32 = jnp.float32
I32 = jnp.int32
_MXU_DTYPE = jnp.bfloat16

D_MODEL = 1024
HEAD_DIM = 64
LANES = 128
POOL_WINDOWS = (2, 4, 8, 16)
POOL_GC = 128
N_HEADS = 8
IDX_HEADS = 8
IDX_DIM = 32
DSA_TOPK = 256
C_KV_GROUPS = 2
HEADS_PER_GROUP = N_HEADS // C_KV_GROUPS
CMP_LEN = 32
CMP_STRIDE = 16
SLC_LEN = 64
SLC_N = 16
WIN = 512
FORCE_BONUS = 1e4
LN_EPS = 1e-5
NEG = -1e30
QBLK = 128
KCHUNK = 512
VMEM_LIMIT = 56 * 1024 * 1024

_IN_WIDTHS = (512, 512, 512, 64, 64, 512, 256, 32, 8, 512, 128, 128, 128, 128, 128, 128, 24, 512, 3072)
_IN_NAMES = ("a_x", "a_z", "b_q", "b_k", "b_v", "b_z", "i_q", "i_k", "i_w", "c_q", "c_kc", "c_vc",
             "c_ks", "c_vs", "c_kw", "c_vw", "c_g", "c_z", "g_merge")
_N_IN = sum(_IN_WIDTHS)
_OFF = dict(zip(_IN_NAMES, np.cumsum((0,) + _IN_WIDTHS[:-1])))
_WID = dict(zip(_IN_NAMES, _IN_WIDTHS))


def _seg(name, lo=0, hi=None):
    hi = _WID[name] if hi is None else hi
    return np.arange(_OFF[name] + lo, _OFF[name] + hi)


def _pad(n):
    return np.full((n,), _N_IN)


def _layout32():
    segs, off, pos = [], {}, 0

    def add(name, idx):
        nonlocal pos
        off[name] = pos
        segs.append(idx)
        pos += len(idx)

    add("a_x", _seg("a_x"))
    add("a_z", _seg("a_z"))
    add("b_z", _seg("b_z"))
    add("c_z", _seg("c_z"))
    add("g_merge", _seg("g_merge"))
    add("c_kc", _seg("c_kc"))
    add("c_vc", _seg("c_vc"))
    add("i_w", np.concatenate([_seg("i_w"), _pad(LANES - 8)]))
    for g in range(C_KV_GROUPS):
        add(f"c_g{g}", np.concatenate([_seg("c_g", 12 * g, 12 * g + 12), _pad(LANES - 12)]))
    idx = np.concatenate(segs)
    return idx, np.ones((len(idx),), np.float32), off


def _layout16():
    segs, scales, off, pos = [], [], {}, 0

    def add(name, idx, scale=1.0):
        nonlocal pos
        off[name] = pos
        segs.append(idx)
        scales.append(np.full((len(idx),), scale, np.float32))
        pos += len(idx)

    qk_scale = HEAD_DIM ** -0.5 * float(np.log2(np.e))
    add("b_q", _seg("b_q"), qk_scale)
    add("c_q", _seg("c_q"), qk_scale)
    add("i_q", _seg("i_q"))
    for name in ("c_ks", "c_kw"):
        for g in range(C_KV_GROUPS):
            one = _seg(name, HEAD_DIM * g, HEAD_DIM * (g + 1))
            add(f"{name}{g}", np.concatenate([one, one]))
    add("b_k", np.concatenate([_seg("b_k")] * 2))
    add("i_k", np.concatenate([_seg("i_k")] * (LANES // IDX_DIM)))
    add("c_vs", _seg("c_vs"))
    add("c_vw", _seg("c_vw"))
    add("b_v", np.concatenate([_seg("b_v"), _pad(LANES - HEAD_DIM)]))
    add("pad", _pad(LANES))
    return np.concatenate(segs), np.concatenate(scales), off


def _gather_cols(a, idx):
    pieces, i = [], 0
    while i < len(idx):
        j = i + 1
        if idx[i] == _N_IN:
            while j < len(idx) and idx[j] == _N_IN:
                j += 1
            pieces.append(jnp.zeros(a.shape[:-1] + (j - i,), a.dtype))
        else:
            while j < len(idx) and idx[j] == idx[j - 1] + 1:
                j += 1
            pieces.append(a[..., int(idx[i]):int(idx[i]) + (j - i)])
        i = j
    return jnp.concatenate(pieces, axis=-1)


_IDX32, _SCALE32, _OFF32 = _layout32()
_IDX16, _SCALE16, _OFF16 = _layout16()
_N32 = len(_IDX32)
_N16 = len(_IDX16)
_TN32 = 640
_TN16 = 512
assert _N32 % _TN32 == 0 and _N16 % _TN16 == 0


def _sigmoid(x):
    return 1.0 / (1.0 + jnp.exp(-x))


def _dot(a, b):
    return jnp.dot(a, b, preferred_element_type=F32)


def _mm_bias_kernel(x_ref, w_ref, b_ref, o_ref):
    o_ref[...] = (_dot(x_ref[...], w_ref[...]) + b_ref[...]).astype(o_ref.dtype)


def _matmul_bias(x, w, b, out_dtype, tm, tn, name):
    m, k = x.shape
    n = w.shape[1]
    return pl.pallas_call(
        _mm_bias_kernel,
        grid=(m // tm, n // tn),
        in_specs=[pl.BlockSpec((tm, k), lambda i, j: (i, 0)),
                  pl.BlockSpec((k, tn), lambda i, j: (0, j)),
                  pl.BlockSpec((1, tn), lambda i, j: (0, j))],
        out_specs=pl.BlockSpec((tm, tn), lambda i, j: (i, j)),
        out_shape=jax.ShapeDtypeStruct((m, n), out_dtype),
        compiler_params=pltpu.CompilerParams(dimension_semantics=("arbitrary", "arbitrary"),
                                             vmem_limit_bytes=VMEM_LIMIT),
        name=name,
    )(x, w, b)


_HALO = 16


def _pool_kernel(xa_ref, halo_ref, az_ref, pw_ref, pb_ref, ps_ref, o_ref, *, tb):
    i = pl.program_id(1)
    cur = xa_ref[0]
    halo = jnp.where(i > 0, halo_ref[0], 0.0)
    ext = jnp.concatenate([halo, cur], axis=0)
    pos = (i * tb + 1 + lax.broadcasted_iota(I32, (tb, 1), 0)).astype(F32)
    outs = []
    for g, wnd in enumerate(POOL_WINDOWS):
        s = ext[:, g * POOL_GC:(g + 1) * POOL_GC]
        k = 1
        while k < wnd:
            s = s + pltpu.roll(s, k, axis=0)
            k *= 2
        mean = s[_HALO:] / jnp.minimum(pos, float(wnd))
        pooled = mean - cur[:, g * POOL_GC:(g + 1) * POOL_GC]
        outs.append(_dot(pooled.astype(_MXU_DTYPE), pw_ref[g]))
    y = jnp.concatenate(outs, axis=1) + pb_ref[...]
    az = az_ref[0]
    o_ref[0] = (y * ps_ref[...] * (az * _sigmoid(az))).astype(o_ref.dtype)


def _pool_mixer(u32, pool_w, pool_b, pool_scale, tb=512):
    b, s, _ = u32.shape
    hb = tb // _HALO
    return pl.pallas_call(
        functools.partial(_pool_kernel, tb=tb),
        grid=(b, s // tb),
        in_specs=[pl.BlockSpec((1, tb, 512), lambda bi, i: (bi, i, _OFF32["a_x"] // 512)),
                  pl.BlockSpec((1, _HALO, 512), lambda bi, i: (bi, jnp.maximum(i * hb - 1, 0), _OFF32["a_x"] // 512)),
                  pl.BlockSpec((1, tb, 512), lambda bi, i: (bi, i, _OFF32["a_z"] // 512)),
                  pl.BlockSpec((4, POOL_GC, POOL_GC), lambda bi, i: (0, 0, 0)),
                  pl.BlockSpec((1, 512), lambda bi, i: (0, 0)),
                  pl.BlockSpec((1, 512), lambda bi, i: (0, 0))],
        out_specs=pl.BlockSpec((1, tb, 512), lambda bi, i: (bi, i, 0)),
        out_shape=jax.ShapeDtypeStruct((b, s, 512), _MXU_DTYPE),
        compiler_params=pltpu.CompilerParams(dimension_semantics=("arbitrary", "arbitrary"),
                                             vmem_limit_bytes=VMEM_LIMIT),
        name="pool_mixer",
    )(u32, u32, u32, pool_w, pool_b, pool_scale)


def _compress_kernel(ch_ref, pos_ref, w1_ref, w2_ref, o_ref):
    ch = ch_ref[0, 0]
    pos = pos_ref[0]
    w1 = w1_ref[0]
    half = ch.shape[1]
    n = ch.shape[0]
    a = _dot((ch + pos[0:1]).astype(_MXU_DTYPE), w1[:half])
    bb = _dot((ch + pos[1:2]).astype(_MXU_DTYPE), w1[half:])
    h = a + pltpu.roll(bb, n - 1, axis=0)
    h = h * _sigmoid(h)
    o_ref[0, 0] = _dot(h.astype(_MXU_DTYPE), w2_ref[0]).astype(o_ref.dtype)


def _compress(chunks, pos, w1, w2dup):
    _, bg, n, width = chunks.shape
    return pl.pallas_call(
        _compress_kernel,
        grid=(2, bg),
        in_specs=[pl.BlockSpec((1, 1, n, width), lambda kv, i: (kv, i, 0, 0)),
                  pl.BlockSpec((1, 2, width), lambda kv, i: (kv, 0, 0)),
                  pl.BlockSpec((1, 2 * width, HEAD_DIM), lambda kv, i: (kv, 0, 0)),
                  pl.BlockSpec((1, HEAD_DIM, LANES), lambda kv, i: (kv, 0, 0))],
        out_specs=pl.BlockSpec((1, 1, n, LANES), lambda kv, i: (kv, i, 0, 0)),
        out_shape=jax.ShapeDtypeStruct((2, bg, n, LANES), _MXU_DTYPE),
        compiler_params=pltpu.CompilerParams(dimension_semantics=("arbitrary", "arbitrary"),
                                             vmem_limit_bytes=VMEM_LIMIT),
        name="nsa_compress",
    )(chunks, pos, w1, w2dup)


KTILE = 128


def _heads_t(qf, n_tiles, width):
    lane = lax.broadcasted_iota(I32, (1, LANES), 1)
    per_tile = LANES // width
    out = []
    for tix in range(n_tiles):
        qt = qf[:, tix * LANES:(tix + 1) * LANES]
        for j in range(per_tile):
            keep = (lane >= j * width) & (lane < (j + 1) * width)
            out.append(jnp.where(keep, qt, 0.0).T.astype(_MXU_DTYPE))
    return jnp.concatenate(out, axis=1)


def _init_state(cols, rows):
    return (jnp.full((1, cols), -1e38, F32), jnp.zeros((1, cols), F32), jnp.zeros((rows, cols), F32))


def _with_mask_rows(q_all):
    cols = q_all.shape[1]
    r = lax.broadcasted_iota(I32, (LANES, cols), 0)
    c = lax.broadcasted_iota(I32, (LANES, cols), 1)
    eye = jnp.where((c & (LANES - 1)) == r, 1.0, 0.0).astype(_MXU_DTYPE)
    return jnp.concatenate([q_all, eye], axis=0)


def _update(s, vt_blk, state):
    m, l, acc = state
    m_new = jnp.maximum(m, jnp.max(s, axis=0, keepdims=True))
    a = jnp.exp2(m - m_new)
    p = jnp.exp2(s - m_new)
    l = a * l + jnp.sum(p, axis=0, keepdims=True)
    acc = a * acc + _dot(vt_blk, p.astype(_MXU_DTYPE))
    return m_new, l, acc


def _scores(k_blk, bias, q_aug):
    return _dot(jnp.concatenate([k_blk, bias], axis=1), q_aug)


def _attend(k_blk, vt_blk, q_aug, bias, state):
    return _update(_scores(k_blk, bias, q_aug), vt_blk, state)


def _attend_chunks(k_ref, vt_ref, bias_of, q_aug, nch, ck, state, s_a, s_b):
    last = nch - 1
    trips = (nch + 1) // 2

    def fill(dst, c):
        cc = jnp.minimum(c, last)
        cb = jnp.minimum(c, 2 * trips - 1)
        dst[...] = _scores(k_ref[pl.ds(pl.multiple_of(cc * ck, ck), ck), :], bias_of(cb), q_aug)

    def body(j, state):
        c = 2 * j
        fill(s_b, c + 1)
        state = _update(s_a[...], vt_ref[c], state)
        fill(s_a, c + 2)
        return _update(s_b[...], vt_ref[jnp.minimum(c + 1, last)], state)

    fill(s_a, 0)
    return lax.fori_loop(0, trips, body, state)


def _finish(state):
    _, l, acc = state
    return acc / l


def _kv_tiles(v, tile):
    b, s, w = v.shape
    c = w // HEAD_DIM
    return v.reshape(b, s // tile, tile, c, HEAD_DIM).transpose(0, 3, 1, 4, 2)


def _nsa_kernel(q_ref, kc_ref, cv_ref, ks_ref, vs_ref, kw_ref, vw_ref, cg_ref, cz_ref, o_ref, selb_ref,
                sa_ref, sb_ref, *, ck, win, n_sel):
    qi = pl.program_id(2)
    q0 = qi * QBLK
    hpg = HEADS_PER_GROUP
    cols = hpg * QBLK
    tq = q0 + lax.broadcasted_iota(I32, (1, QBLK), 1)
    q_all = _with_mask_rows(_heads_t(q_ref[0].astype(F32), hpg // 2, HEAD_DIM))

    ncp = kc_ref.shape[2]
    n_s = cv_ref.shape[1] - HEAD_DIM
    cend = lax.broadcasted_iota(I32, (ncp, 1), 0) * CMP_STRIDE + (CMP_LEN - 1)
    out = _finish(_attend(kc_ref[0, 0], cv_ref[0], q_all, jnp.where(cend <= tq, 0.0, NEG).astype(_MXU_DTYPE),
                          _init_state(cols, HEAD_DIM + n_s)))
    seen = jnp.concatenate([tq >= CMP_LEN - 1] * hpg, axis=1)
    out = jnp.where(seen, out, 0.0)
    o_c = out[:HEAD_DIM]
    work = out[HEAD_DIM:, 0:QBLK]
    for r in range(1, hpg):
        work = work + out[HEAD_DIM:, r * QBLK:(r + 1) * QBLK]

    span = win + QBLK
    start = pl.multiple_of(jnp.maximum(q0 - win, 0), QBLK)
    kpos = start + lax.broadcasted_iota(I32, (span, 1), 0)
    bias = jnp.where((kpos <= tq) & (kpos > tq - win), 0.0, NEG).astype(_MXU_DTYPE)
    vw = jnp.concatenate([vw_ref[0, 0, start // KTILE + i] for i in range(span // KTILE)], axis=1)
    o_w = _finish(_attend(kw_ref[0, pl.ds(start, span), :], vw, q_all, bias, _init_state(cols, HEAD_DIM)))

    ji = lax.broadcasted_iota(I32, (n_s, 1), 0)
    jf = ji.astype(F32)
    blk = lax.shift_right_logical(tq, 6)
    forced = (ji == 0) | (ji == blk) | (ji == blk - 1)
    work = work + jnp.where(forced, FORCE_BONUS, 0.0)
    work = jnp.where(ji * SLC_LEN <= tq, work, NEG)
    sel = jnp.zeros((n_s, QBLK), F32)
    for _ in range(n_sel):
        mx = jnp.max(work, axis=0, keepdims=True)
        first = jnp.min(jnp.where(work == mx, jf, float(n_s)), axis=0, keepdims=True)
        hit = jf == first
        sel = jnp.where(hit, 1.0, sel)
        work = jnp.where(hit, -3e38, work)
    selb_ref[0:n_s, :] = jnp.where(sel > 0.5, 0.0, NEG)
    selb_ref[n_s:, :] = jnp.full((selb_ref.shape[0] - n_s, QBLK), NEG, F32)

    per_chunk = ck // SLC_LEN
    krow = lax.broadcasted_iota(I32, (ck, 1), 0)

    def sel_bias(c):
        rows = [jnp.broadcast_to(selb_ref[pl.ds(c * per_chunk + i, 1), :], (SLC_LEN, QBLK))
                for i in range(per_chunk)]
        return jnp.where(c * ck + krow <= tq, jnp.concatenate(rows, axis=0), NEG).astype(_MXU_DTYPE)

    o_s = _finish(_attend_chunks(ks_ref.at[0], vs_ref.at[0, 0], sel_bias, q_all, q0 // ck + 1, ck,
                                 _init_state(cols, HEAD_DIM), sa_ref, sb_ref))

    gate = _sigmoid(cg_ref[0].T[0:16])
    ys = []
    for r in range(hpg):
        c_ = slice(r * QBLK, (r + 1) * QBLK)
        ys.append(gate[3 * r:3 * r + 1] * o_c[:, c_] + gate[3 * r + 1:3 * r + 2] * o_s[:, c_]
                  + gate[3 * r + 2:3 * r + 3] * o_w[:, c_])
    tiles = [jnp.concatenate(ys[i:i + 2], axis=0).T for i in range(0, hpg, 2)]
    cz = cz_ref[0]
    o_ref[0] = (jnp.concatenate(tiles, axis=1) * (cz * _sigmoid(cz))).astype(o_ref.dtype)


def _nsa_mixer(u16, u32, kvcmp, overlap_t):
    b, s, _ = u16.shape
    ck = min(KCHUNK, s)
    gw = HEADS_PER_GROUP * HEAD_DIM
    bg, ncp = kvcmp.shape[1], kvcmp.shape[2]
    n_s = overlap_t.shape[0]
    vs_t = _kv_tiles(u16[:, :, _OFF16["c_vs"]:_OFF16["c_vs"] + LANES], ck)
    vw_t = _kv_tiles(u16[:, :, _OFF16["c_vw"]:_OFF16["c_vw"] + LANES], KTILE)
    vc_t = kvcmp[1][:, :, :HEAD_DIM].transpose(0, 2, 1)
    cval = jnp.concatenate([vc_t, jnp.broadcast_to(overlap_t[None], (bg, n_s, ncp))], axis=1)
    rows_c = HEAD_DIM + n_s

    def k_spec(name):
        base = _OFF16[name + "0"] // LANES
        return pl.BlockSpec((1, s, LANES), lambda bi, g, qi: (bi, 0, base + g))

    def vt_spec(tile):
        return pl.BlockSpec((1, 1, s // tile, HEAD_DIM, tile), lambda bi, g, qi: (bi, g, 0, 0, 0))

    return pl.pallas_call(
        functools.partial(_nsa_kernel, ck=ck, win=WIN, n_sel=min(SLC_N, n_s)),
        grid=(b, C_KV_GROUPS, s // QBLK),
        in_specs=[pl.BlockSpec((1, QBLK, gw), lambda bi, g, qi: (bi, qi, _OFF16["c_q"] // gw + g)),
                  pl.BlockSpec((1, 1, ncp, LANES), lambda bi, g, qi: (0, bi * C_KV_GROUPS + g, 0, 0)),
                  pl.BlockSpec((1, rows_c, ncp), lambda bi, g, qi: (bi * C_KV_GROUPS + g, 0, 0)),
                  k_spec("c_ks"), vt_spec(ck), k_spec("c_kw"), vt_spec(KTILE),
                  pl.BlockSpec((1, QBLK, LANES), lambda bi, g, qi: (bi, qi, _OFF32["c_g0"] // LANES + g)),
                  pl.BlockSpec((1, QBLK, gw), lambda bi, g, qi: (bi, qi, _OFF32["c_z"] // gw + g))],
        out_specs=pl.BlockSpec((1, QBLK, gw), lambda bi, g, qi: (bi, qi, g)),
        out_shape=jax.ShapeDtypeStruct((b, s, N_HEADS * HEAD_DIM), _MXU_DTYPE),
        scratch_shapes=[pltpu.VMEM((n_s + ck // SLC_LEN, QBLK), F32),
                        pltpu.VMEM((ck, HEADS_PER_GROUP * QBLK), F32),
                        pltpu.VMEM((ck, HEADS_PER_GROUP * QBLK), F32)],
        compiler_params=pltpu.CompilerParams(dimension_semantics=("arbitrary", "arbitrary", "arbitrary"),
                                             vmem_limit_bytes=VMEM_LIMIT),
        name="nsa_mixer",
    )(u16, kvcmp, cval, u16, vs_t, u16, vw_t, u32, u32)


def _dsa_kernel(q_ref, iq_ref, k_ref, v_ref, ik_ref, iw_ref, bz_ref, o_ref, keys_ref, bias_ref, cut_ref,
                sa_ref, sb_ref, *, ck, topk, nbits):
    qi = pl.program_id(1)
    q0 = qi * QBLK
    nch = q0 // ck + 1
    tq = q0 + lax.broadcasted_iota(I32, (1, QBLK), 1)
    krow = lax.broadcasted_iota(I32, (ck, 1), 0)
    srow = lax.broadcasted_iota(I32, (8, 1), 0)

    iq_all = _heads_t(iq_ref[0].astype(F32), IDX_HEADS * IDX_DIM // LANES, IDX_DIM)
    iw_t = (iw_ref[0] * (IDX_HEADS ** -0.5 * IDX_DIM ** -0.5)).T

    def fill_rel(dst, c):
        cc = jnp.minimum(c, nch - 1)
        dst[...] = _dot(ik_ref[0, pl.ds(pl.multiple_of(cc * ck, ck), ck), :], iq_all)

    def keys_from(src, c):
        rel = src[...]
        sc = jnp.maximum(rel[:, 0:QBLK], 0.0) * iw_t[0:1]
        for h in range(1, IDX_HEADS):
            sc = sc + jnp.maximum(rel[:, h * QBLK:(h + 1) * QBLK], 0.0) * iw_t[h:h + 1]
        sc = jnp.where(c * ck + krow <= tq, sc, NEG)
        sc = jnp.where(sc == 0.0, 0.0, sc)
        bits = pltpu.bitcast(sc, I32)
        keys_ref[c] = bits ^ (lax.shift_right_arithmetic(bits, 31) & 0x7FFFFFFF)

    def score_body(j, _):
        c = 2 * j
        fill_rel(sb_ref, c + 1)
        keys_from(sa_ref, c)
        fill_rel(sa_ref, c + 2)
        keys_from(sb_ref, c + 1)
        return 0

    fill_rel(sa_ref, 0)
    lax.fori_loop(0, (nch + 1) // 2, score_body, 0)

    n_acc = 4

    def count(pred):
        def body(c, accs):
            accs = list(accs)
            for r in range(ck // 8):
                k = keys_ref[c, r * 8:(r + 1) * 8, :]
                accs[r % n_acc] = accs[r % n_acc] + jnp.where(pred(k, c * ck + r * 8 + srow), 1.0, 0.0)
            return tuple(accs)
        accs = lax.fori_loop(0, nch, body, tuple(jnp.zeros((8, LANES), F32) for _ in range(n_acc)))
        return jnp.sum(sum(accs[1:], accs[0]), axis=0, keepdims=True)

    def bit_body(i, thr):
        cand = thr + lax.shift_left(jnp.int32(1), 31 - i)
        return jnp.where(count(lambda k, _: k >= cand) >= topk, cand, thr)

    thr = lax.fori_loop(0, 32, bit_body, jnp.full((1, LANES), -2 ** 31, I32))
    c_gt = count(lambda k, _: k > thr)
    c_eq = count(lambda k, _: k >= thr) - c_gt
    need = topk - c_gt

    cut_ref[...] = jnp.full(cut_ref.shape, 2 ** nbits, I32)

    @pl.when(jnp.max(jnp.where(c_eq > need, 1.0, 0.0)) > 0.0)
    def _():
        def tie_body(i, cut):
            cand = cut + lax.shift_left(jnp.int32(1), nbits - 1 - i)
            below = count(lambda k, kpos: (k == thr) & (kpos < cand))
            return jnp.where(below < need, cand, cut)
        cut = lax.fori_loop(0, nbits, tie_body, jnp.zeros((1, LANES), I32))
        cut_ref[...] = jnp.broadcast_to(cut, cut_ref.shape)

    cut = cut_ref[0:1, :]

    def bias_body(c, _):
        k = keys_ref[jnp.minimum(c, nch - 1)]
        kpos = c * ck + krow
        chosen = (k > thr) | ((k == thr) & (kpos <= cut))
        bias_ref[c] = jnp.where(chosen & (kpos <= tq), 0.0, NEG).astype(bias_ref.dtype)
        return 0

    lax.fori_loop(0, 2 * ((nch + 1) // 2), bias_body, 0)

    q_all = _with_mask_rows(_heads_t(q_ref[0].astype(F32), N_HEADS * HEAD_DIM // LANES, HEAD_DIM))

    out = _finish(_attend_chunks(k_ref.at[0], v_ref.at[0, 0], lambda c: bias_ref[c], q_all, nch, ck,
                                 _init_state(N_HEADS * QBLK, HEAD_DIM), sa_ref, sb_ref))
    tiles = [jnp.concatenate([out[:, h * QBLK:(h + 1) * QBLK], out[:, (h + 1) * QBLK:(h + 2) * QBLK]], axis=0).T
             for h in range(0, N_HEADS, 2)]
    bz = bz_ref[0]
    o_ref[0] = (jnp.concatenate(tiles, axis=1) * (bz * _sigmoid(bz))).astype(o_ref.dtype)


def _dsa_mixer(u16, u32):
    b, s, _ = u16.shape
    ck = min(KCHUNK, s)
    w = N_HEADS * HEAD_DIM
    iqw = IDX_HEADS * IDX_DIM
    v_t = _kv_tiles(u16[:, :, _OFF16["b_v"]:_OFF16["b_v"] + HEAD_DIM], ck)

    def k_spec(name):
        return pl.BlockSpec((1, s, LANES), lambda bi, qi: (bi, 0, _OFF16[name] // LANES))

    return pl.pallas_call(
        functools.partial(_dsa_kernel, ck=ck, topk=min(DSA_TOPK, s // 4), nbits=int(s).bit_length()),
        grid=(b, s // QBLK),
        in_specs=[pl.BlockSpec((1, QBLK, w), lambda bi, qi: (bi, qi, _OFF16["b_q"] // w)),
                  pl.BlockSpec((1, QBLK, iqw), lambda bi, qi: (bi, qi, _OFF16["i_q"] // iqw)),
                  k_spec("b_k"),
                  pl.BlockSpec((1, 1, s // ck, HEAD_DIM, ck), lambda bi, qi: (bi, 0, 0, 0, 0)),
                  k_spec("i_k"),
                  pl.BlockSpec((1, QBLK, LANES), lambda bi, qi: (bi, qi, _OFF32["i_w"] // LANES)),
                  pl.BlockSpec((1, QBLK, w), lambda bi, qi: (bi, qi, _OFF32["b_z"] // w))],
        out_specs=pl.BlockSpec((1, QBLK, w), lambda bi, qi: (bi, qi, 0)),
        out_shape=jax.ShapeDtypeStruct((b, s, w), _MXU_DTYPE),
        scratch_shapes=[pltpu.VMEM((s // ck + s // ck % 2, ck, QBLK), I32),
                        pltpu.VMEM((s // ck + s // ck % 2, ck, QBLK), _MXU_DTYPE),
                        pltpu.VMEM((8, LANES), I32),
                        pltpu.VMEM((ck, N_HEADS * QBLK), F32),
                        pltpu.VMEM((ck, N_HEADS * QBLK), F32)],
        compiler_params=pltpu.CompilerParams(dimension_semantics=("arbitrary", "arbitrary"),
                                             vmem_limit_bytes=VMEM_LIMIT),
        name="dsa_mixer",
    )(u16, u16, u16, v_t, u16, u32, u32)


def _merge_kernel(x_ref, ya_ref, yb_ref, yc_ref, g0_ref, g1_ref, g2_ref, wa_ref, wb_ref, wc_ref, wo_ref,
                  lg_ref, lb_ref, o32_ref, o16_ref, *, alpha):
    m = (_sigmoid(g0_ref[...]) * _dot(ya_ref[...], wa_ref[...])
         + _sigmoid(g1_ref[...]) * _dot(yb_ref[...], wb_ref[...])
         + _sigmoid(g2_ref[...]) * _dot(yc_ref[...], wc_ref[...]))
    z = alpha * x_ref[...] + _dot(m.astype(_MXU_DTYPE), wo_ref[...])
    mu = jnp.mean(z, axis=1, keepdims=True)
    zc = z - mu
    var = jnp.mean(zc * zc, axis=1, keepdims=True)
    y = zc * lax.rsqrt(var + LN_EPS) * lg_ref[...] + lb_ref[...]
    o32_ref[...] = y
    o16_ref[...] = y.astype(o16_ref.dtype)


def _merge(x, ya, yb, yc, u32, wa, wb, wc, wo, lg, lb, alpha, tm=512):
    m, d = x.shape
    w = ya.shape[1]
    gbase = _OFF32["g_merge"] // d
    row = lambda i: (i, 0)
    const = lambda i: (0, 0)
    return pl.pallas_call(
        functools.partial(_merge_kernel, alpha=alpha),
        grid=(m // tm,),
        in_specs=[pl.BlockSpec((tm, d), row),
                  pl.BlockSpec((tm, w), row), pl.BlockSpec((tm, w), row), pl.BlockSpec((tm, w), row),
                  pl.BlockSpec((tm, d), lambda i: (i, gbase)),
                  pl.BlockSpec((tm, d), lambda i: (i, gbase + 1)),
                  pl.BlockSpec((tm, d), lambda i: (i, gbase + 2)),
                  pl.BlockSpec((w, d), const), pl.BlockSpec((w, d), const), pl.BlockSpec((w, d), const),
                  pl.BlockSpec((d, d), const), pl.BlockSpec((1, d), const), pl.BlockSpec((1, d), const)],
        out_specs=[pl.BlockSpec((tm, d), row), pl.BlockSpec((tm, d), row)],
        out_shape=[jax.ShapeDtypeStruct((m, d), F32), jax.ShapeDtypeStruct((m, d), _MXU_DTYPE)],
        compiler_params=pltpu.CompilerParams(dimension_semantics=("arbitrary",), vmem_limit_bytes=VMEM_LIMIT),
        name="merge_out_ln",
    )(x, ya, yb, yc, u32, u32, u32, wa, wb, wc, wo, lg, lb)


def _overlap_matrix(ncp, s):
    n_c = (s - CMP_LEN) // CMP_STRIDE + 1
    c_start = np.arange(ncp) * CMP_STRIDE
    s_start = np.arange(s // SLC_LEN) * SLC_LEN
    ov = (c_start[:, None] <= s_start[None, :] + SLC_LEN - 1) & (c_start[:, None] + CMP_LEN - 1 >= s_start[None, :])
    ov &= (np.arange(ncp) < n_c)[:, None]
    return ov.astype(np.float32)


def _layer(x32, x16, p, consts, alpha):
    b, s, d = x32.shape
    m = b * s
    x16 = x16.reshape(m, d)
    u32 = _matmul_bias(x16, p["w32"], p["b32"], F32, min(2048, m), _TN32, "in_proj_f32").reshape(b, s, _N32)
    u16 = _matmul_bias(x16, p["w16"], p["b16"], _MXU_DTYPE, min(2048, m), _TN16, "in_proj_bf16").reshape(b, s, _N16)

    y_a = _pool_mixer(u32, p["pool_w"], p["pool_b"], p["pool_scale"])

    nch16 = s // CMP_STRIDE
    tok = u32[:, :, _OFF32["c_kc"]:_OFF32["c_kc"] + 2 * LANES]
    chunks = (tok.reshape(b, nch16, CMP_STRIDE, 2, C_KV_GROUPS, HEAD_DIM)
              .transpose(3, 0, 4, 1, 2, 5).reshape(2, b * C_KV_GROUPS, nch16, CMP_STRIDE * HEAD_DIM))
    kvcmp = _compress(chunks, p["cmp_pos"], p["cmp_w1"], p["cmp_w2"])

    y_c = _nsa_mixer(u16, u32, kvcmp, consts["overlap_t"])
    y_b = _dsa_mixer(u16, u32)

    x32n, x16n = _merge(x32.reshape(m, d), y_a.reshape(m, -1), y_b.reshape(m, -1), y_c.reshape(m, -1),
                        u32.reshape(m, _N32), p["w_pa"], p["w_pb"], p["w_pc"], p["w_o"], p["ln_g"], p["ln_b"], alpha)
    return x32n.reshape(b, s, d), x16n.reshape(b, s, d)


def _prepare_params(w_in, b_in, pool_w, pool_b, pool_scale, cmp_pos_k, cmp_pos_v, cmp_w1_k, cmp_w2_k,
                    cmp_w1_v, cmp_w2_v, w_proj_a, w_proj_b, w_proj_c, w_o, ln_g, ln_b):
    nl = w_in.shape[0]
    mx = _MXU_DTYPE
    half = CMP_LEN // 2
    pos = jnp.stack([cmp_pos_k, cmp_pos_v], axis=1).reshape(nl, 2, 2, half * HEAD_DIM)
    return {
        "w32": (_gather_cols(w_in, _IDX32) * _SCALE32).astype(mx),
        "b32": (_gather_cols(b_in, _IDX32) * _SCALE32)[:, None, :],
        "w16": (_gather_cols(w_in, _IDX16) * _SCALE16).astype(mx),
        "b16": (_gather_cols(b_in, _IDX16) * _SCALE16)[:, None, :],
        "pool_w": pool_w.astype(mx),
        "pool_b": pool_b.reshape(nl, 1, -1),
        "pool_scale": pool_scale.reshape(nl, 1, -1),
        "cmp_pos": pos,
        "cmp_w1": jnp.stack([cmp_w1_k, cmp_w1_v], axis=1).astype(mx),
        "cmp_w2": jnp.concatenate([jnp.stack([cmp_w2_k, cmp_w2_v], axis=1)] * 2, axis=-1).astype(mx),
        "w_pa": w_proj_a.astype(mx), "w_pb": w_proj_b.astype(mx), "w_pc": w_proj_c.astype(mx),
        "w_o": w_o.astype(mx),
        "ln_g": ln_g[:, None, :], "ln_b": ln_b[:, None, :],
    }


def kernel(x, w_in, b_in, pool_w, pool_b, pool_scale, cmp_pos_k, cmp_pos_v, cmp_w1_k, cmp_w2_k, cmp_w1_v, cmp_w2_v, w_proj_a, w_proj_b, w_proj_c, w_o, ln_g, ln_b):
    depth = w_in.shape[0]
    s = x.shape[1]
    params = _prepare_params(w_in, b_in, pool_w, pool_b, pool_scale, cmp_pos_k, cmp_pos_v, cmp_w1_k, cmp_w2_k,
                             cmp_w1_v, cmp_w2_v, w_proj_a, w_proj_b, w_proj_c, w_o, ln_g, ln_b)
    consts = {"overlap_t": jnp.asarray(_overlap_matrix(s // CMP_STRIDE, s).T, _MXU_DTYPE)}
    alpha = (2 * depth) ** 0.25
    h32, h16 = x, x.astype(_MXU_DTYPE)
    for l in range(depth):
        h32, h16 = _layer(h32, h16, {k: v[l] for k, v in params.items()}, consts, alpha)
    return h32
```

```python
import functools

import numpy as np
import jax
import jax.numpy as jnp
from jax import lax
from jax.experimental import pallas as pl
from jax.experimental.pallas import tpu as pltpu

F32 = jnp.float32
I32 = jnp.int32
_MXU_DTYPE = jnp.bfloat16

D_MODEL = 1024
HEAD_DIM = 64
LANES = 128
POOL_WINDOWS = (2, 4, 8, 16)
POOL_GC = 128
N_HEADS = 8
IDX_HEADS = 8
IDX_DIM = 32
DSA_TOPK = 256
C_KV_GROUPS = 2
HEADS_PER_GROUP = N_HEADS // C_KV_GROUPS
CMP_LEN = 32
CMP_STRIDE = 16
SLC_LEN = 64
SLC_N = 16
WIN = 512
FORCE_BONUS = 1e4
LN_EPS = 1e-5
NEG = -1e30
QBLK = 128
KCHUNK = 512
VMEM_LIMIT = 56 * 1024 * 1024

_IN_WIDTHS = (512, 512, 512, 64, 64, 512, 256, 32, 8, 512, 128, 128, 128, 128, 128, 128, 24, 512, 3072)
_IN_NAMES = ("a_x", "a_z", "b_q", "b_k", "b_v", "b_z", "i_q", "i_k", "i_w", "c_q", "c_kc", "c_vc",
             "c_ks", "c_vs", "c_kw", "c_vw", "c_g", "c_z", "g_merge")
_N_IN = sum(_IN_WIDTHS)
_OFF = dict(zip(_IN_NAMES, np.cumsum((0,) + _IN_WIDTHS[:-1])))
_WID = dict(zip(_IN_NAMES, _IN_WIDTHS))


def _seg(name, lo=0, hi=None):
    hi = _WID[name] if hi is None else hi
    return np.arange(_OFF[name] + lo, _OFF[name] + hi)


def _pad(n):
    return np.full((n,), _N_IN)


def _layout32():
    segs, off, pos = [], {}, 0

    def add(name, idx):
        nonlocal pos
        off[name] = pos
        segs.append(idx)
        pos += len(idx)

    add("a_x", _seg("a_x"))
    add("a_z", _seg("a_z"))
    add("b_z", _seg("b_z"))
    add("c_z", _seg("c_z"))
    add("g_merge", _seg("g_merge"))
    add("c_kc", _seg("c_kc"))
    add("c_vc", _seg("c_vc"))
    add("i_w", np.concatenate([_seg("i_w"), _pad(LANES - 8)]))
    for g in range(C_KV_GROUPS):
        add(f"c_g{g}", np.concatenate([_seg("c_g", 12 * g, 12 * g + 12), _pad(LANES - 12)]))
    idx = np.concatenate(segs)
    return idx, np.ones((len(idx),), np.float32), off


def _layout16():
    segs, scales, off, pos = [], [], {}, 0

    def add(name, idx, scale=1.0):
        nonlocal pos
        off[name] = pos
        segs.append(idx)
        scales.append(np.full((len(idx),), scale, np.float32))
        pos += len(idx)

    qk_scale = HEAD_DIM ** -0.5 * float(np.log2(np.e))
    add("b_q", _seg("b_q"), qk_scale)
    add("c_q", _seg("c_q"), qk_scale)
    add("i_q", _seg("i_q"))
    for name in ("c_ks", "c_kw"):
        for g in range(C_KV_GROUPS):
            one = _seg(name, HEAD_DIM * g, HEAD_DIM * (g + 1))
            add(f"{name}{g}", np.concatenate([one, one]))
    add("b_k", np.concatenate([_seg("b_k")] * 2))
    add("i_k", np.concatenate([_seg("i_k")] * (LANES // IDX_DIM)))
    add("c_vs", _seg("c_vs"))
    add("c_vw", _seg("c_vw"))
    add("b_v", np.concatenate([_seg("b_v"), _pad(LANES - HEAD_DIM)]))
    add("pad", _pad(LANES))
    return np.concatenate(segs), np.concatenate(scales), off


def _gather_cols(a, idx):
    pieces, i = [], 0
    while i < len(idx):
        j = i + 1
        if idx[i] == _N_IN:
            while j < len(idx) and idx[j] == _N_IN:
                j += 1
            pieces.append(jnp.zeros(a.shape[:-1] + (j - i,), a.dtype))
        else:
            while j < len(idx) and idx[j] == idx[j - 1] + 1:
                j += 1
            pieces.append(a[..., int(idx[i]):int(idx[i]) + (j - i)])
        i = j
    return jnp.concatenate(pieces, axis=-1)


_IDX32, _SCALE32, _OFF32 = _layout32()
_IDX16, _SCALE16, _OFF16 = _layout16()
_N32 = len(_IDX32)
_N16 = len(_IDX16)
_TN32 = 640
_TN16 = 512
assert _N32 % _TN32 == 0 and _N16 % _TN16 == 0


def _sigmoid(x):
    return 1.0 / (1.0 + jnp.exp(-x))


def _dot(a, b):
    return jnp.dot(a, b, preferred_element_type=F32)


def _mm_bias_kernel(x_ref, w_ref, b_ref, o_ref):
    o_ref[...] = (_dot(x_ref[...], w_ref[...]) + b_ref[...]).astype(o_ref.dtype)


def _matmul_bias(x, w, b, out_dtype, tm, tn, name):
    m, k = x.shape
    n = w.shape[1]
    return pl.pallas_call(
        _mm_bias_kernel,
        grid=(m // tm, n // tn),
        in_specs=[pl.BlockSpec((tm, k), lambda i, j: (i, 0)),
                  pl.BlockSpec((k, tn), lambda i, j: (0, j)),
                  pl.BlockSpec((1, tn), lambda i, j: (0, j))],
        out_specs=pl.BlockSpec((tm, tn), lambda i, j: (i, j)),
        out_shape=jax.ShapeDtypeStruct((m, n), out_dtype),
        compiler_params=pltpu.CompilerParams(dimension_semantics=("arbitrary", "arbitrary"),
                                             vmem_limit_bytes=VMEM_LIMIT),
        name=name,
    )(x, w, b)


_HALO = 16


def _pool_kernel(xa_ref, halo_ref, az_ref, pw_ref, pb_ref, ps_ref, o_ref, *, tb):
    i = pl.program_id(1)
    cur = xa_ref[0]
    halo = jnp.where(i > 0, halo_ref[0], 0.0)
    ext = jnp.concatenate([halo, cur], axis=0)
    pos = (i * tb + 1 + lax.broadcasted_iota(I32, (tb, 1), 0)).astype(F32)
    outs = []
    for g, wnd in enumerate(POOL_WINDOWS):
        s = ext[:, g * POOL_GC:(g + 1) * POOL_GC]
        k = 1
        while k < wnd:
            s = s + pltpu.roll(s, k, axis=0)
            k *= 2
        mean = s[_HALO:] / jnp.minimum(pos, float(wnd))
        pooled = mean - cur[:, g * POOL_GC:(g + 1) * POOL_GC]
        outs.append(_dot(pooled.astype(_MXU_DTYPE), pw_ref[g]))
    y = jnp.concatenate(outs, axis=1) + pb_ref[...]
    az = az_ref[0]
    o_ref[0] = (y * ps_ref[...] * (az * _sigmoid(az))).astype(o_ref.dtype)


def _pool_mixer(u32, pool_w, pool_b, pool_scale, tb=512):
    b, s, _ = u32.shape
    hb = tb // _HALO
    return pl.pallas_call(
        functools.partial(_pool_kernel, tb=tb),
        grid=(b, s // tb),
        in_specs=[pl.BlockSpec((1, tb, 512), lambda bi, i: (bi, i, _OFF32["a_x"] // 512)),
                  pl.BlockSpec((1, _HALO, 512), lambda bi, i: (bi, jnp.maximum(i * hb - 1, 0), _OFF32["a_x"] // 512)),
                  pl.BlockSpec((1, tb, 512), lambda bi, i: (bi, i, _OFF32["a_z"] // 512)),
                  pl.BlockSpec((4, POOL_GC, POOL_GC), lambda bi, i: (0, 0, 0)),
                  pl.BlockSpec((1, 512), lambda bi, i: (0, 0)),
                  pl.BlockSpec((1, 512), lambda bi, i: (0, 0))],
        out_specs=pl.BlockSpec((1, tb, 512), lambda bi, i: (bi, i, 0)),
        out_shape=jax.ShapeDtypeStruct((b, s, 512), _MXU_DTYPE),
        compiler_params=pltpu.CompilerParams(dimension_semantics=("arbitrary", "arbitrary"),
                                             vmem_limit_bytes=VMEM_LIMIT),
        name="pool_mixer",
    )(u32, u32, u32, pool_w, pool_b, pool_scale)


def _compress_kernel(ch_ref, pos_ref, w1_ref, w2_ref, o_ref):
    ch = ch_ref[0, 0]
    pos = pos_ref[0]
    w1 = w1_ref[0]
    half = ch.shape[1]
    n = ch.shape[0]
    a = _dot((ch + pos[0:1]).astype(_MXU_DTYPE), w1[:half])
    bb = _dot((ch + pos[1:2]).astype(_MXU_DTYPE), w1[half:])
    h = a + pltpu.roll(bb, n - 1, axis=0)
    h = h * _sigmoid(h)
    o_ref[0, 0] = _dot(h.astype(_MXU_DTYPE), w2_ref[0]).astype(o_ref.dtype)


def _compress(chunks, pos, w1, w2dup):
    _, bg, n, width = chunks.shape
    return pl.pallas_call(
        _compress_kernel,
        grid=(2, bg),
        in_specs=[pl.BlockSpec((1, 1, n, width), lambda kv, i: (kv, i, 0, 0)),
                  pl.BlockSpec((1, 2, width), lambda kv, i: (kv, 0, 0)),
                  pl.BlockSpec((1, 2 * width, HEAD_DIM), lambda kv, i: (kv, 0, 0)),
                  pl.BlockSpec((1, HEAD_DIM, LANES), lambda kv, i: (kv, 0, 0))],
        out_specs=pl.BlockSpec((1, 1, n, LANES), lambda kv, i: (kv, i, 0, 0)),
        out_shape=jax.ShapeDtypeStruct((2, bg, n, LANES), _MXU_DTYPE),
        compiler_params=pltpu.CompilerParams(dimension_semantics=("arbitrary", "arbitrary"),
                                             vmem_limit_bytes=VMEM_LIMIT),
        name="nsa_compress",
    )(chunks, pos, w1, w2dup)


KTILE = 128


def _heads_t(qf, n_tiles, width):
    lane = lax.broadcasted_iota(I32, (1, LANES), 1)
    per_tile = LANES // width
    out = []
    for tix in range(n_tiles):
        qt = qf[:, tix * LANES:(tix + 1) * LANES]
        for j in range(per_tile):
            keep = (lane >= j * width) & (lane < (j + 1) * width)
            out.append(jnp.where(keep, qt, 0.0).T.astype(_MXU_DTYPE))
    return jnp.concatenate(out, axis=1)


def _init_state(cols, rows):
    return (jnp.full((1, cols), -1e38, F32), jnp.zeros((1, cols), F32), jnp.zeros((rows, cols), F32))


def _with_mask_rows(q_all):
    cols = q_all.shape[1]
    r = lax.broadcasted_iota(I32, (LANES, cols), 0)
    c = lax.broadcasted_iota(I32, (LANES, cols), 1)
    eye = jnp.where((c & (LANES - 1)) == r, 1.0, 0.0).astype(_MXU_DTYPE)
    return jnp.concatenate([q_all, eye], axis=0)


def _update(s, vt_blk, state):
    m, l, acc = state
    m_new = jnp.maximum(m, jnp.max(s, axis=0, keepdims=True))
    a = jnp.exp2(m - m_new)
    p = jnp.exp2(s - m_new)
    l = a * l + jnp.sum(p, axis=0, keepdims=True)
    acc = a * acc + _dot(vt_blk, p.astype(_MXU_DTYPE))
    return m_new, l, acc


def _scores(k_blk, bias, q_aug):
    return _dot(jnp.concatenate([k_blk, bias], axis=1), q_aug)


def _attend(k_blk, vt_blk, q_aug, bias, state):
    return _update(_scores(k_blk, bias, q_aug), vt_blk, state)


def _attend_chunks(streams, nch, ck):
    last = nch - 1
    trips = (nch + 1) // 2

    def fill(st, dst, c):
        k_of, _, bias_of, q_aug = st[:4]
        dst[...] = _scores(k_of(jnp.minimum(c, last)), bias_of(jnp.minimum(c, 2 * trips - 1)), q_aug)

    def body(j, states):
        c = 2 * j
        for st in streams:
            fill(st, st[6], c + 1)
        states = [_update(st[5][...], st[1][c], state) for st, state in zip(streams, states)]
        for st in streams:
            fill(st, st[5], c + 2)
        return tuple(_update(st[6][...], st[1][jnp.minimum(c + 1, last)], state)
                     for st, state in zip(streams, states))

    for st in streams:
        fill(st, st[5], 0)
    return lax.fori_loop(0, trips, body, tuple(st[4] for st in streams))


def _finish(state):
    _, l, acc = state
    return acc / l


def _kv_tiles(v, tile):
    b, s, w = v.shape
    c = w // HEAD_DIM
    return v.reshape(b, s // tile, tile, c, HEAD_DIM).transpose(0, 3, 1, 4, 2)


def _nsa_kernel(q_ref, kc_ref, cv_ref, ks_ref, vs_ref, kw_ref, vw_ref, cg0_ref, cg1_ref, cz_ref, o_ref, selb_ref,
                sa_ref, sb_ref, *, ck, win, n_sel):
    qi = pl.program_id(1)
    q0 = qi * QBLK
    hpg = HEADS_PER_GROUP
    cols = hpg * QBLK
    gw = hpg * HEAD_DIM
    tq = q0 + lax.broadcasted_iota(I32, (1, QBLK), 1)
    ncp = kc_ref.shape[3]
    n_s = cv_ref.shape[2] - HEAD_DIM
    qf = q_ref[0].astype(F32)
    q_all = [_with_mask_rows(_heads_t(qf[:, g * gw:(g + 1) * gw], hpg // 2, HEAD_DIM))
             for g in range(C_KV_GROUPS)]
    lanes = lambda g: slice(g * LANES, (g + 1) * LANES)

    cend = lax.broadcasted_iota(I32, (ncp, 1), 0) * CMP_STRIDE + (CMP_LEN - 1)
    bias_c = jnp.where(cend <= tq, 0.0, NEG).astype(_MXU_DTYPE)
    seen = jnp.concatenate([tq >= CMP_LEN - 1] * hpg, axis=1)
    outs_c = [jnp.where(seen, _finish(_attend(kc_ref[0, 0, g], cv_ref[0, g], q_all[g], bias_c,
                                              _init_state(cols, HEAD_DIM + n_s))), 0.0)
              for g in range(C_KV_GROUPS)]

    span = win + QBLK
    start = pl.multiple_of(jnp.maximum(q0 - win, 0), QBLK)
    kpos = start + lax.broadcasted_iota(I32, (span, 1), 0)
    bias_w = jnp.where((kpos <= tq) & (kpos > tq - win), 0.0, NEG).astype(_MXU_DTYPE)
    o_w = []
    for g in range(C_KV_GROUPS):
        vw = jnp.concatenate([vw_ref[0, g, start // KTILE + i] for i in range(span // KTILE)], axis=1)
        o_w.append(_finish(_attend(kw_ref[0, pl.ds(start, span), lanes(g)], vw, q_all[g], bias_w,
                                   _init_state(cols, HEAD_DIM))))

    ji = lax.broadcasted_iota(I32, (n_s, 1), 0)
    jf = ji.astype(F32)
    blk = lax.shift_right_logical(tq, 6)
    forced = (ji == 0) | (ji == blk) | (ji == blk - 1)
    for g in range(C_KV_GROUPS):
        out = outs_c[g]
        work = out[HEAD_DIM:, 0:QBLK]
        for r in range(1, hpg):
            work = work + out[HEAD_DIM:, r * QBLK:(r + 1) * QBLK]
        work = work + jnp.where(forced, FORCE_BONUS, 0.0)
        work = jnp.where(ji * SLC_LEN <= tq, work, NEG)
        sel = jnp.zeros((n_s, QBLK), F32)
        for _ in range(n_sel):
            mx = jnp.max(work, axis=0, keepdims=True)
            first = jnp.min(jnp.where(work == mx, jf, float(n_s)), axis=0, keepdims=True)
            hit = jf == first
            sel = jnp.where(hit, 1.0, sel)
            work = jnp.where(hit, -3e38, work)
        selb_ref[g, 0:n_s, :] = jnp.where(sel > 0.5, 0.0, NEG)
        selb_ref[g, n_s:, :] = jnp.full((selb_ref.shape[1] - n_s, QBLK), NEG, F32)

    per_chunk = ck // SLC_LEN
    krow = lax.broadcasted_iota(I32, (ck, 1), 0)

    def stream(g):
        def sel_bias(c):
            rows = [jnp.broadcast_to(selb_ref[g, pl.ds(c * per_chunk + i, 1), :], (SLC_LEN, QBLK))
                    for i in range(per_chunk)]
            return jnp.where(c * ck + krow <= tq, jnp.concatenate(rows, axis=0), NEG).astype(_MXU_DTYPE)

        def k_of(c):
            return ks_ref[0, pl.ds(pl.multiple_of(c * ck, ck), ck), lanes(g)]

        return (k_of, vs_ref.at[0, g], sel_bias, q_all[g], _init_state(cols, HEAD_DIM), sa_ref.at[g], sb_ref.at[g])

    o_s = [_finish(st) for st in _attend_chunks([stream(g) for g in range(C_KV_GROUPS)], q0 // ck + 1, ck)]

    tiles = []
    for g, cg_ref in enumerate((cg0_ref, cg1_ref)):
        gate = _sigmoid(cg_ref[0].T[0:16])
        ys = []
        for r in range(hpg):
            c_ = slice(r * QBLK, (r + 1) * QBLK)
            ys.append(gate[3 * r:3 * r + 1] * outs_c[g][:HEAD_DIM, c_] + gate[3 * r + 1:3 * r + 2] * o_s[g][:, c_]
                      + gate[3 * r + 2:3 * r + 3] * o_w[g][:, c_])
        tiles += [jnp.concatenate(ys[i:i + 2], axis=0).T for i in range(0, hpg, 2)]
    cz = cz_ref[0]
    o_ref[0] = (jnp.concatenate(tiles, axis=1) * (cz * _sigmoid(cz))).astype(o_ref.dtype)


def _nsa_mixer(u16, u32, kvcmp, overlap_t):
    b, s, _ = u16.shape
    ck = min(KCHUNK, s)
    gn = C_KV_GROUPS
    w = N_HEADS * HEAD_DIM
    ncp = kvcmp.shape[2]
    n_s = overlap_t.shape[0]
    vs_t = _kv_tiles(u16[:, :, _OFF16["c_vs"]:_OFF16["c_vs"] + LANES], ck)
    vw_t = _kv_tiles(u16[:, :, _OFF16["c_vw"]:_OFF16["c_vw"] + LANES], KTILE)
    kv5 = kvcmp.reshape(2, b, gn, ncp, LANES)
    vc_t = kv5[1][:, :, :, :HEAD_DIM].transpose(0, 1, 3, 2)
    cval = jnp.concatenate([vc_t, jnp.broadcast_to(overlap_t[None, None], (b, gn, n_s, ncp))], axis=2)
    rows_c = HEAD_DIM + n_s
    kw2 = gn * LANES

    def vt_spec(tile):
        return pl.BlockSpec((1, gn, s // tile, HEAD_DIM, tile), lambda bi, qi: (bi, 0, 0, 0, 0))

    return pl.pallas_call(
        functools.partial(_nsa_kernel, ck=ck, win=WIN, n_sel=min(SLC_N, n_s)),
        grid=(b, s // QBLK),
        in_specs=[pl.BlockSpec((1, QBLK, w), lambda bi, qi: (bi, qi, _OFF16["c_q"] // w)),
                  pl.BlockSpec((1, 1, gn, ncp, LANES), lambda bi, qi: (0, bi, 0, 0, 0)),
                  pl.BlockSpec((1, gn, rows_c, ncp), lambda bi, qi: (bi, 0, 0, 0)),
                  pl.BlockSpec((1, s, kw2), lambda bi, qi: (bi, 0, _OFF16["c_ks0"] // kw2)), vt_spec(ck),
                  pl.BlockSpec((1, s, kw2), lambda bi, qi: (bi, 0, _OFF16["c_kw0"] // kw2)), vt_spec(KTILE),
                  pl.BlockSpec((1, QBLK, LANES), lambda bi, qi: (bi, qi, _OFF32["c_g0"] // LANES)),
                  pl.BlockSpec((1, QBLK, LANES), lambda bi, qi: (bi, qi, _OFF32["c_g1"] // LANES)),
                  pl.BlockSpec((1, QBLK, w), lambda bi, qi: (bi, qi, _OFF32["c_z"] // w))],
        out_specs=pl.BlockSpec((1, QBLK, w), lambda bi, qi: (bi, qi, 0)),
        out_shape=jax.ShapeDtypeStruct((b, s, w), _MXU_DTYPE),
        scratch_shapes=[pltpu.VMEM((gn, n_s + ck // SLC_LEN, QBLK), F32),
                        pltpu.VMEM((gn, ck, HEADS_PER_GROUP * QBLK), F32),
                        pltpu.VMEM((gn, ck, HEADS_PER_GROUP * QBLK), F32)],
        compiler_params=pltpu.CompilerParams(dimension_semantics=("arbitrary", "arbitrary"),
                                             vmem_limit_bytes=VMEM_LIMIT),
        name="nsa_mixer",
    )(u16, kv5, cval, u16, vs_t, u16, vw_t, u32, u32, u32)


def _dsa_kernel(q_ref, iq_ref, k_ref, v_ref, ik_ref, iw_ref, bz_ref, o_ref, keys_ref, bias_ref, cut_ref,
                sa_ref, sb_ref, *, ck, topk, nbits):
    qi = pl.program_id(1)
    q0 = qi * QBLK
    nch = q0 // ck + 1
    tq = q0 + lax.broadcasted_iota(I32, (1, QBLK), 1)
    krow = lax.broadcasted_iota(I32, (ck, 1), 0)
    srow = lax.broadcasted_iota(I32, (8, 1), 0)

    iq_all = _heads_t(iq_ref[0].astype(F32), IDX_HEADS * IDX_DIM // LANES, IDX_DIM)
    iw_t = (iw_ref[0] * (IDX_HEADS ** -0.5 * IDX_DIM ** -0.5)).T

    def fill_rel(dst, c):
        cc = jnp.minimum(c, nch - 1)
        dst[...] = _dot(ik_ref[0, pl.ds(pl.multiple_of(cc * ck, ck), ck), :], iq_all)

    def keys_from(src, c):
        rel = src[...]
        sc = jnp.maximum(rel[:, 0:QBLK], 0.0) * iw_t[0:1]
        for h in range(1, IDX_HEADS):
            sc = sc + jnp.maximum(rel[:, h * QBLK:(h + 1) * QBLK], 0.0) * iw_t[h:h + 1]
        sc = jnp.where(c * ck + krow <= tq, sc, NEG)
        sc = jnp.where(sc == 0.0, 0.0, sc)
        bits = pltpu.bitcast(sc, I32)
        keys_ref[c] = bits ^ (lax.shift_right_arithmetic(bits, 31) & 0x7FFFFFFF)

    def score_body(j, _):
        c = 2 * j
        fill_rel(sb_ref, c + 1)
        keys_from(sa_ref, c)
        fill_rel(sa_ref, c + 2)
        keys_from(sb_ref, c + 1)
        return 0

    fill_rel(sa_ref, 0)
    lax.fori_loop(0, (nch + 1) // 2, score_body, 0)

    n_acc = 4

    def count(pred):
        def body(c, accs):
            accs = list(accs)
            for r in range(ck // 8):
                k = keys_ref[c, r * 8:(r + 1) * 8, :]
                accs[r % n_acc] = accs[r % n_acc] + jnp.where(pred(k, c * ck + r * 8 + srow), 1.0, 0.0)
            return tuple(accs)
        accs = lax.fori_loop(0, nch, body, tuple(jnp.zeros((8, LANES), F32) for _ in range(n_acc)))
        return jnp.sum(sum(accs[1:], accs[0]), axis=0, keepdims=True)

    def bit_body(i, thr):
        cand = thr + lax.shift_left(jnp.int32(1), 31 - i)
        return jnp.where(count(lambda k, _: k >= cand) >= topk, cand, thr)

    thr = lax.fori_loop(0, 32, bit_body, jnp.full((1, LANES), -2 ** 31, I32))
    c_gt = count(lambda k, _: k > thr)
    c_eq = count(lambda k, _: k >= thr) - c_gt
    need = topk - c_gt

    cut_ref[...] = jnp.full(cut_ref.shape, 2 ** nbits, I32)

    @pl.when(jnp.max(jnp.where(c_eq > need, 1.0, 0.0)) > 0.0)
    def _():
        def tie_body(i, cut):
            cand = cut + lax.shift_left(jnp.int32(1), nbits - 1 - i)
            below = count(lambda k, kpos: (k == thr) & (kpos < cand))
            return jnp.where(below < need, cand, cut)
        cut = lax.fori_loop(0, nbits, tie_body, jnp.zeros((1, LANES), I32))
        cut_ref[...] = jnp.broadcast_to(cut, cut_ref.shape)

    cut = cut_ref[0:1, :]

    def bias_body(c, _):
        k = keys_ref[jnp.minimum(c, nch - 1)]
        kpos = c * ck + krow
        chosen = (k > thr) | ((k == thr) & (kpos <= cut))
        bias_ref[c] = jnp.where(chosen & (kpos <= tq), 0.0, NEG).astype(bias_ref.dtype)
        return 0

    lax.fori_loop(0, 2 * ((nch + 1) // 2), bias_body, 0)

    q_all = _with_mask_rows(_heads_t(q_ref[0].astype(F32), N_HEADS * HEAD_DIM // LANES, HEAD_DIM))

    dsa_stream = (lambda c: k_ref[0, pl.ds(pl.multiple_of(c * ck, ck), ck), :], v_ref.at[0, 0], lambda c: bias_ref[c],
                  q_all, _init_state(N_HEADS * QBLK, HEAD_DIM), sa_ref, sb_ref)
    out = _finish(_attend_chunks([dsa_stream], nch, ck)[0])
    tiles = [jnp.concatenate([out[:, h * QBLK:(h + 1) * QBLK], out[:, (h + 1) * QBLK:(h + 2) * QBLK]], axis=0).T
             for h in range(0, N_HEADS, 2)]
    bz = bz_ref[0]
    o_ref[0] = (jnp.concatenate(tiles, axis=1) * (bz * _sigmoid(bz))).astype(o_ref.dtype)


def _dsa_mixer(u16, u32):
    b, s, _ = u16.shape
    ck = min(KCHUNK, s)
    w = N_HEADS * HEAD_DIM
    iqw = IDX_HEADS * IDX_DIM
    v_t = _kv_tiles(u16[:, :, _OFF16["b_v"]:_OFF16["b_v"] + HEAD_DIM], ck)

    def k_spec(name):
        return pl.BlockSpec((1, s, LANES), lambda bi, qi: (bi, 0, _OFF16[name] // LANES))

    return pl.pallas_call(
        functools.partial(_dsa_kernel, ck=ck, topk=min(DSA_TOPK, s // 4), nbits=int(s).bit_length()),
        grid=(b, s // QBLK),
        in_specs=[pl.BlockSpec((1, QBLK, w), lambda bi, qi: (bi, qi, _OFF16["b_q"] // w)),
                  pl.BlockSpec((1, QBLK, iqw), lambda bi, qi: (bi, qi, _OFF16["i_q"] // iqw)),
                  k_spec("b_k"),
                  pl.BlockSpec((1, 1, s // ck, HEAD_DIM, ck), lambda bi, qi: (bi, 0, 0, 0, 0)),
                  k_spec("i_k"),
                  pl.BlockSpec((1, QBLK, LANES), lambda bi, qi: (bi, qi, _OFF32["i_w"] // LANES)),
                  pl.BlockSpec((1, QBLK, w), lambda bi, qi: (bi, qi, _OFF32["b_z"] // w))],
        out_specs=pl.BlockSpec((1, QBLK, w), lambda bi, qi: (bi, qi, 0)),
        out_shape=jax.ShapeDtypeStruct((b, s, w), _MXU_DTYPE),
        scratch_shapes=[pltpu.VMEM((s // ck + s // ck % 2, ck, QBLK), I32),
                        pltpu.VMEM((s // ck + s // ck % 2, ck, QBLK), _MXU_DTYPE),
                        pltpu.VMEM((8, LANES), I32),
                        pltpu.VMEM((ck, N_HEADS * QBLK), F32),
                        pltpu.VMEM((ck, N_HEADS * QBLK), F32)],
        compiler_params=pltpu.CompilerParams(dimension_semantics=("arbitrary", "arbitrary"),
                                             vmem_limit_bytes=VMEM_LIMIT),
        name="dsa_mixer",
    )(u16, u16, u16, v_t, u16, u32, u32)


def _merge_kernel(x_ref, ya_ref, yb_ref, yc_ref, g0_ref, g1_ref, g2_ref, wa_ref, wb_ref, wc_ref, wo_ref,
                  lg_ref, lb_ref, o32_ref, o16_ref, *, alpha):
    m = (_sigmoid(g0_ref[...]) * _dot(ya_ref[...], wa_ref[...])
         + _sigmoid(g1_ref[...]) * _dot(yb_ref[...], wb_ref[...])
         + _sigmoid(g2_ref[...]) * _dot(yc_ref[...], wc_ref[...]))
    z = alpha * x_ref[...] + _dot(m.astype(_MXU_DTYPE), wo_ref[...])
    mu = jnp.mean(z, axis=1, keepdims=True)
    zc = z - mu
    var = jnp.mean(zc * zc, axis=1, keepdims=True)
    y = zc * lax.rsqrt(var + LN_EPS) * lg_ref[...] + lb_ref[...]
    o32_ref[...] = y
    o16_ref[...] = y.astype(o16_ref.dtype)


def _merge(x, ya, yb, yc, u32, wa, wb, wc, wo, lg, lb, alpha, tm=512):
    m, d = x.shape
    w = ya.shape[1]
    gbase = _OFF32["g_merge"] // d
    row = lambda i: (i, 0)
    const = lambda i: (0, 0)
    return pl.pallas_call(
        functools.partial(_merge_kernel, alpha=alpha),
        grid=(m // tm,),
        in_specs=[pl.BlockSpec((tm, d), row),
                  pl.BlockSpec((tm, w), row), pl.BlockSpec((tm, w), row), pl.BlockSpec((tm, w), row),
                  pl.BlockSpec((tm, d), lambda i: (i, gbase)),
                  pl.BlockSpec((tm, d), lambda i: (i, gbase + 1)),
                  pl.BlockSpec((tm, d), lambda i: (i, gbase + 2)),
                  pl.BlockSpec((w, d), const), pl.BlockSpec((w, d), const), pl.BlockSpec((w, d), const),
                  pl.BlockSpec((d, d), const), pl.BlockSpec((1, d), const), pl.BlockSpec((1, d), const)],
        out_specs=[pl.BlockSpec((tm, d), row), pl.BlockSpec((tm, d), row)],
        out_shape=[jax.ShapeDtypeStruct((m, d), F32), jax.ShapeDtypeStruct((m, d), _MXU_DTYPE)],
        compiler_params=pltpu.CompilerParams(dimension_semantics=("arbitrary",), vmem_limit_bytes=VMEM_LIMIT),
        name="merge_out_ln",
    )(x, ya, yb, yc, u32, u32, u32, wa, wb, wc, wo, lg, lb)


def _overlap_matrix(ncp, s):
    n_c = (s - CMP_LEN) // CMP_STRIDE + 1
    c_start = np.arange(ncp) * CMP_STRIDE
    s_start = np.arange(s // SLC_LEN) * SLC_LEN
    ov = (c_start[:, None] <= s_start[None, :] + SLC_LEN - 1) & (c_start[:, None] + CMP_LEN - 1 >= s_start[None, :])
    ov &= (np.arange(ncp) < n_c)[:, None]
    return ov.astype(np.float32)


def _layer(x32, x16, p, consts, alpha):
    b, s, d = x32.shape
    m = b * s
    x16 = x16.reshape(m, d)
    u32 = _matmul_bias(x16, p["w32"], p["b32"], F32, min(2048, m), _TN32, "in_proj_f32").reshape(b, s, _N32)
    u16 = _matmul_bias(x16, p["w16"], p["b16"], _MXU_DTYPE, min(2048, m), _TN16, "in_proj_bf16").reshape(b, s, _N16)

    y_a = _pool_mixer(u32, p["pool_w"], p["pool_b"], p["pool_scale"])

    nch16 = s // CMP_STRIDE
    tok = u32[:, :, _OFF32["c_kc"]:_OFF32["c_kc"] + 2 * LANES]
    chunks = (tok.reshape(b, nch16, CMP_STRIDE, 2, C_KV_GROUPS, HEAD_DIM)
              .transpose(3, 0, 4, 1, 2, 5).reshape(2, b * C_KV_GROUPS, nch16, CMP_STRIDE * HEAD_DIM))
    kvcmp = _compress(chunks, p["cmp_pos"], p["cmp_w1"], p["cmp_w2"])

    y_c = _nsa_mixer(u16, u32, kvcmp, consts["overlap_t"])
    y_b = _dsa_mixer(u16, u32)

    x32n, x16n = _merge(x32.reshape(m, d), y_a.reshape(m, -1), y_b.reshape(m, -1), y_c.reshape(m, -1),
                        u32.reshape(m, _N32), p["w_pa"], p["w_pb"], p["w_pc"], p["w_o"], p["ln_g"], p["ln_b"], alpha)
    return x32n.reshape(b, s, d), x16n.reshape(b, s, d)


def _prepare_params(w_in, b_in, pool_w, pool_b, pool_scale, cmp_pos_k, cmp_pos_v, cmp_w1_k, cmp_w2_k,
                    cmp_w1_v, cmp_w2_v, w_proj_a, w_proj_b, w_proj_c, w_o, ln_g, ln_b):
    nl = w_in.shape[0]
    mx = _MXU_DTYPE
    half = CMP_LEN // 2
    pos = jnp.stack([cmp_pos_k, cmp_pos_v], axis=1).reshape(nl, 2, 2, half * HEAD_DIM)
    return {
        "w32": (_gather_cols(w_in, _IDX32) * _SCALE32).astype(mx),
        "b32": (_gather_cols(b_in, _IDX32) * _SCALE32)[:, None, :],
        "w16": (_gather_cols(w_in, _IDX16) * _SCALE16).astype(mx),
        "b16": (_gather_cols(b_in, _IDX16) * _SCALE16)[:, None, :],
        "pool_w": pool_w.astype(mx),
        "pool_b": pool_b.reshape(nl, 1, -1),
        "pool_scale": pool_scale.reshape(nl, 1, -1),
        "cmp_pos": pos,
        "cmp_w1": jnp.stack([cmp_w1_k, cmp_w1_v], axis=1).astype(mx),
        "cmp_w2": jnp.concatenate([jnp.stack([cmp_w2_k, cmp_w2_v], axis=1)] * 2, axis=-1).astype(mx),
        "w_pa": w_proj_a.astype(mx), "w_pb": w_proj_b.astype(mx), "w_pc": w_proj_c.astype(mx),
        "w_o": w_o.astype(mx),
        "ln_g": ln_g[:, None, :], "ln_b": ln_b[:, None, :],
    }


def kernel(x, w_in, b_in, pool_w, pool_b, pool_scale, cmp_pos_k, cmp_pos_v, cmp_w1_k, cmp_w2_k, cmp_w1_v, cmp_w2_v, w_proj_a, w_proj_b, w_proj_c, w_o, ln_g, ln_b):
    depth = w_in.shape[0]
    s = x.shape[1]
    params = _prepare_params(w_in, b_in, pool_w, pool_b, pool_scale, cmp_pos_k, cmp_pos_v, cmp_w1_k, cmp_w2_k,
                             cmp_w1_v, cmp_w2_v, w_proj_a, w_proj_b, w_proj_c, w_o, ln_g, ln_b)
    consts = {"overlap_t": jnp.asarray(_overlap_matrix(s // CMP_STRIDE, s).T, _MXU_DTYPE)}
    alpha = (2 * depth) ** 0.25
    h32, h16 = x, x.astype(_MXU_DTYPE)
    for l in range(depth):
        h32, h16 = _layer(h32, h16, {k: v[l] for k, v in params.items()}, consts, alpha)
    return h32
```

```python
import functools

import numpy as np
import jax
import jax.numpy as jnp
from jax import lax
from jax.experimental import pallas as pl
from jax.experimental.pallas import tpu as pltpu

F32 = jnp.float32
I32 = jnp.int32
_MXU_DTYPE = jnp.bfloat16

D_MODEL = 1024
HEAD_DIM = 64
LANES = 128
POOL_WINDOWS = (2, 4, 8, 16)
POOL_GC = 128
N_HEADS = 8
IDX_HEADS = 8
IDX_DIM = 32
DSA_TOPK = 256
C_KV_GROUPS = 2
HEADS_PER_GROUP = N_HEADS // C_KV_GROUPS
CMP_LEN = 32
CMP_STRIDE = 16
SLC_LEN = 64
SLC_N = 16
WIN = 512
FORCE_BONUS = 1e4
LN_EPS = 1e-5
NEG = -1e30
QBLK = 128
KCHUNK = 512
VMEM_LIMIT = 56 * 1024 * 1024

_IN_WIDTHS = (512, 512, 512, 64, 64, 512, 256, 32, 8, 512, 128, 128, 128, 128, 128, 128, 24, 512, 3072)
_IN_NAMES = ("a_x", "a_z", "b_q", "b_k", "b_v", "b_z", "i_q", "i_k", "i_w", "c_q", "c_kc", "c_vc",
             "c_ks", "c_vs", "c_kw", "c_vw", "c_g", "c_z", "g_merge")
_N_IN = sum(_IN_WIDTHS)
_OFF = dict(zip(_IN_NAMES, np.cumsum((0,) + _IN_WIDTHS[:-1])))
_WID = dict(zip(_IN_NAMES, _IN_WIDTHS))


def _seg(name, lo=0, hi=None):
    hi = _WID[name] if hi is None else hi
    return np.arange(_OFF[name] + lo, _OFF[name] + hi)


def _pad(n):
    return np.full((n,), _N_IN)


def _layout32():
    segs, off, pos = [], {}, 0

    def add(name, idx):
        nonlocal pos
        off[name] = pos
        segs.append(idx)
        pos += len(idx)

    add("a_x", _seg("a_x"))
    add("a_z", _seg("a_z"))
    add("b_z", _seg("b_z"))
    add("c_z", _seg("c_z"))
    add("g_merge", _seg("g_merge"))
    add("c_kc", _seg("c_kc"))
    add("c_vc", _seg("c_vc"))
    add("i_w", np.concatenate([_seg("i_w"), _pad(LANES - 8)]))
    for g in range(C_KV_GROUPS):
        add(f"c_g{g}", np.concatenate([_seg("c_g", 12 * g, 12 * g + 12), _pad(LANES - 12)]))
    idx = np.concatenate(segs)
    return idx, np.ones((len(idx),), np.float32), off


def _layout16():
    segs, scales, off, pos = [], [], {}, 0

    def add(name, idx, scale=1.0):
        nonlocal pos
        off[name] = pos
        segs.append(idx)
        scales.append(np.full((len(idx),), scale, np.float32))
        pos += len(idx)

    qk_scale = HEAD_DIM ** -0.5 * float(np.log2(np.e))
    add("b_q", _seg("b_q"), qk_scale)
    add("c_q", _seg("c_q"), qk_scale)
    add("i_q", _seg("i_q"))
    for name in ("c_ks", "c_kw"):
        for g in range(C_KV_GROUPS):
            one = _seg(name, HEAD_DIM * g, HEAD_DIM * (g + 1))
            add(f"{name}{g}", np.concatenate([one, one]))
    add("b_k", np.concatenate([_seg("b_k")] * 2))
    add("i_k", np.concatenate([_seg("i_k")] * (LANES // IDX_DIM)))
    add("c_vs", _seg("c_vs"))
    add("c_vw", _seg("c_vw"))
    add("b_v", np.concatenate([_seg("b_v"), _pad(LANES - HEAD_DIM)]))
    add("pad", _pad(LANES))
    return np.concatenate(segs), np.concatenate(scales), off


def _gather_cols(a, idx):
    pieces, i = [], 0
    while i < len(idx):
        j = i + 1
        if idx[i] == _N_IN:
            while j < len(idx) and idx[j] == _N_IN:
                j += 1
            pieces.append(jnp.zeros(a.shape[:-1] + (j - i,), a.dtype))
        else:
            while j < len(idx) and idx[j] == idx[j - 1] + 1:
                j += 1
            pieces.append(a[..., int(idx[i]):int(idx[i]) + (j - i)])
        i = j
    return jnp.concatenate(pieces, axis=-1)


_IDX32, _SCALE32, _OFF32 = _layout32()
_IDX16, _SCALE16, _OFF16 = _layout16()
_N32 = len(_IDX32)
_N16 = len(_IDX16)
_TN32 = 640
_TN16 = 512
assert _N32 % _TN32 == 0 and _N16 % _TN16 == 0


def _sigmoid(x):
    return 1.0 / (1.0 + jnp.exp(-x))


def _dot(a, b):
    return jnp.dot(a, b, preferred_element_type=F32)


def _mm_bias_kernel(x_ref, w_ref, b_ref, o_ref):
    o_ref[...] = (_dot(x_ref[...], w_ref[...]) + b_ref[...]).astype(o_ref.dtype)


def _matmul_bias(x, w, b, out_dtype, tm, tn, name):
    m, k = x.shape
    n = w.shape[1]
    return pl.pallas_call(
        _mm_bias_kernel,
        grid=(m // tm, n // tn),
        in_specs=[pl.BlockSpec((tm, k), lambda i, j: (i, 0)),
                  pl.BlockSpec((k, tn), lambda i, j: (0, j)),
                  pl.BlockSpec((1, tn), lambda i, j: (0, j))],
        out_specs=pl.BlockSpec((tm, tn), lambda i, j: (i, j)),
        out_shape=jax.ShapeDtypeStruct((m, n), out_dtype),
        compiler_params=pltpu.CompilerParams(dimension_semantics=("arbitrary", "arbitrary"),
                                             vmem_limit_bytes=VMEM_LIMIT),
        name=name,
    )(x, w, b)


_HALO = 16


def _pool_kernel(xa_ref, halo_ref, az_ref, pw_ref, pb_ref, ps_ref, o_ref, *, tb):
    i = pl.program_id(1)
    cur = xa_ref[0]
    halo = jnp.where(i > 0, halo_ref[0], 0.0)
    ext = jnp.concatenate([halo, cur], axis=0)
    pos = (i * tb + 1 + lax.broadcasted_iota(I32, (tb, 1), 0)).astype(F32)
    outs = []
    for g, wnd in enumerate(POOL_WINDOWS):
        s = ext[:, g * POOL_GC:(g + 1) * POOL_GC]
        k = 1
        while k < wnd:
            s = s + pltpu.roll(s, k, axis=0)
            k *= 2
        mean = s[_HALO:] / jnp.minimum(pos, float(wnd))
        pooled = mean - cur[:, g * POOL_GC:(g + 1) * POOL_GC]
        outs.append(_dot(pooled.astype(_MXU_DTYPE), pw_ref[g]))
    y = jnp.concatenate(outs, axis=1) + pb_ref[...]
    az = az_ref[0]
    o_ref[0] = (y * ps_ref[...] * (az * _sigmoid(az))).astype(o_ref.dtype)


def _pool_mixer(u32, pool_w, pool_b, pool_scale, tb=512):
    b, s, _ = u32.shape
    hb = tb // _HALO
    return pl.pallas_call(
        functools.partial(_pool_kernel, tb=tb),
        grid=(b, s // tb),
        in_specs=[pl.BlockSpec((1, tb, 512), lambda bi, i: (bi, i, _OFF32["a_x"] // 512)),
                  pl.BlockSpec((1, _HALO, 512), lambda bi, i: (bi, jnp.maximum(i * hb - 1, 0), _OFF32["a_x"] // 512)),
                  pl.BlockSpec((1, tb, 512), lambda bi, i: (bi, i, _OFF32["a_z"] // 512)),
                  pl.BlockSpec((4, POOL_GC, POOL_GC), lambda bi, i: (0, 0, 0)),
                  pl.BlockSpec((1, 512), lambda bi, i: (0, 0)),
                  pl.BlockSpec((1, 512), lambda bi, i: (0, 0))],
        out_specs=pl.BlockSpec((1, tb, 512), lambda bi, i: (bi, i, 0)),
        out_shape=jax.ShapeDtypeStruct((b, s, 512), _MXU_DTYPE),
        compiler_params=pltpu.CompilerParams(dimension_semantics=("arbitrary", "arbitrary"),
                                             vmem_limit_bytes=VMEM_LIMIT),
        name="pool_mixer",
    )(u32, u32, u32, pool_w, pool_b, pool_scale)


def _compress_kernel(ch_ref, pos_ref, w1_ref, w2_ref, o_ref):
    ch = ch_ref[0, 0]
    pos = pos_ref[0]
    w1 = w1_ref[0]
    half = ch.shape[1]
    n = ch.shape[0]
    a = _dot((ch + pos[0:1]).astype(_MXU_DTYPE), w1[:half])
    bb = _dot((ch + pos[1:2]).astype(_MXU_DTYPE), w1[half:])
    h = a + pltpu.roll(bb, n - 1, axis=0)
    h = h * _sigmoid(h)
    o_ref[0, 0] = _dot(h.astype(_MXU_DTYPE), w2_ref[0]).astype(o_ref.dtype)


def _compress(chunks, pos, w1, w2dup):
    _, bg, n, width = chunks.shape
    return pl.pallas_call(
        _compress_kernel,
        grid=(2, bg),
        in_specs=[pl.BlockSpec((1, 1, n, width), lambda kv, i: (kv, i, 0, 0)),
                  pl.BlockSpec((1, 2, width), lambda kv, i: (kv, 0, 0)),
                  pl.BlockSpec((1, 2 * width, HEAD_DIM), lambda kv, i: (kv, 0, 0)),
                  pl.BlockSpec((1, HEAD_DIM, LANES), lambda kv, i: (kv, 0, 0))],
        out_specs=pl.BlockSpec((1, 1, n, LANES), lambda kv, i: (kv, i, 0, 0)),
        out_shape=jax.ShapeDtypeStruct((2, bg, n, LANES), _MXU_DTYPE),
        compiler_params=pltpu.CompilerParams(dimension_semantics=("arbitrary", "arbitrary"),
                                             vmem_limit_bytes=VMEM_LIMIT),
        name="nsa_compress",
    )(chunks, pos, w1, w2dup)


def _heads_t(qf, n_tiles, width):
    lane = lax.broadcasted_iota(I32, (1, LANES), 1)
    per_tile = LANES // width
    out = []
    for tix in range(n_tiles):
        qt = qf[:, tix * LANES:(tix + 1) * LANES]
        for j in range(per_tile):
            keep = (lane >= j * width) & (lane < (j + 1) * width)
            out.append(jnp.where(keep, qt, 0.0).T.astype(_MXU_DTYPE))
    return jnp.concatenate(out, axis=1)


def _init_state(cols, rows):
    return (jnp.full((1, cols), -1e38, F32), jnp.zeros((1, cols), F32), jnp.zeros((rows, cols), F32))


def _with_mask_rows(q_all):
    cols = q_all.shape[1]
    r = lax.broadcasted_iota(I32, (LANES, cols), 0)
    c = lax.broadcasted_iota(I32, (LANES, cols), 1)
    eye = jnp.where((c & (LANES - 1)) == r, 1.0, 0.0).astype(_MXU_DTYPE)
    return jnp.concatenate([q_all, eye], axis=0)


def _update(s, vt_blk, state):
    m, l, acc = state
    m_new = jnp.maximum(m, jnp.max(s, axis=0, keepdims=True))
    a = jnp.exp2(m - m_new)
    p = jnp.exp2(s - m_new)
    l = a * l + jnp.sum(p, axis=0, keepdims=True)
    acc = a * acc + lax.dot_general(vt_blk, p.astype(_MXU_DTYPE), (((0,), (0,)), ((), ())),
                                    preferred_element_type=F32)
    return m_new, l, acc


def _scores(k_blk, bias, q_aug):
    return _dot(jnp.concatenate([k_blk, bias], axis=1), q_aug)


def _attend(k_blk, vt_blk, q_aug, bias, state):
    return _update(_scores(k_blk, bias, q_aug), vt_blk, state)


def _attend_chunks(streams, nch, ck):
    last = nch - 1
    trips = (nch + 1) // 2

    def fill(st, dst, c):
        k_of, _, bias_of, q_aug = st[:4]
        dst[...] = _scores(k_of(jnp.minimum(c, last)), bias_of(jnp.minimum(c, 2 * trips - 1)), q_aug)

    def body(j, states):
        c = 2 * j
        for st in streams:
            fill(st, st[6], c + 1)
        states = [_update(st[5][...], st[1](c), state) for st, state in zip(streams, states)]
        for st in streams:
            fill(st, st[5], c + 2)
        return tuple(_update(st[6][...], st[1](jnp.minimum(c + 1, last)), state)
                     for st, state in zip(streams, states))

    for st in streams:
        fill(st, st[5], 0)
    return lax.fori_loop(0, trips, body, tuple(st[4] for st in streams))


def _finish(state):
    _, l, acc = state
    return acc / l


def _nsa_kernel(q_ref, kc_ref, cv_ref, ks_ref, vs_ref, kw_ref, vw_ref, cg0_ref, cg1_ref, cz_ref, o_ref, selb_ref,
                sa_ref, sb_ref, *, ck, win, n_sel):
    qi = pl.program_id(1)
    q0 = qi * QBLK
    hpg = HEADS_PER_GROUP
    cols = hpg * QBLK
    gw = hpg * HEAD_DIM
    tq = q0 + lax.broadcasted_iota(I32, (1, QBLK), 1)
    ncp = kc_ref.shape[3]
    n_s = cv_ref.shape[3] - LANES
    qf = q_ref[0].astype(F32)
    q_all = [_with_mask_rows(_heads_t(qf[:, g * gw:(g + 1) * gw], hpg // 2, HEAD_DIM))
             for g in range(C_KV_GROUPS)]
    lanes = lambda g: slice(g * LANES, (g + 1) * LANES)

    cend = lax.broadcasted_iota(I32, (ncp, 1), 0) * CMP_STRIDE + (CMP_LEN - 1)
    bias_c = jnp.where(cend <= tq, 0.0, NEG).astype(_MXU_DTYPE)
    seen = jnp.concatenate([tq >= CMP_LEN - 1] * hpg, axis=1)
    outs_c = [jnp.where(seen, _finish(_attend(kc_ref[0, 0, g], cv_ref[0, g], q_all[g], bias_c,
                                              _init_state(cols, LANES + n_s))), 0.0)
              for g in range(C_KV_GROUPS)]

    span = win + QBLK
    start = pl.multiple_of(jnp.maximum(q0 - win, 0), QBLK)
    kpos = start + lax.broadcasted_iota(I32, (span, 1), 0)
    bias_w = jnp.where((kpos <= tq) & (kpos > tq - win), 0.0, NEG).astype(_MXU_DTYPE)
    o_w = []
    for g in range(C_KV_GROUPS):
        o_w.append(_finish(_attend(kw_ref[0, pl.ds(start, span), lanes(g)], vw_ref[0, pl.ds(start, span), :],
                                   q_all[g], bias_w, _init_state(cols, LANES))))

    ji = lax.broadcasted_iota(I32, (n_s, 1), 0)
    jf = ji.astype(F32)
    blk = lax.shift_right_logical(tq, 6)
    forced = (ji == 0) | (ji == blk) | (ji == blk - 1)
    for g in range(C_KV_GROUPS):
        out = outs_c[g]
        work = out[LANES:, 0:QBLK]
        for r in range(1, hpg):
            work = work + out[LANES:, r * QBLK:(r + 1) * QBLK]
        work = work + jnp.where(forced, FORCE_BONUS, 0.0)
        work = jnp.where(ji * SLC_LEN <= tq, work, NEG)
        sel = jnp.zeros((n_s, QBLK), F32)
        for _ in range(n_sel):
            mx = jnp.max(work, axis=0, keepdims=True)
            first = jnp.min(jnp.where(work == mx, jf, float(n_s)), axis=0, keepdims=True)
            hit = jf == first
            sel = jnp.where(hit, 1.0, sel)
            work = jnp.where(hit, -3e38, work)
        selb_ref[g, 0:n_s, :] = jnp.where(sel > 0.5, 0.0, NEG)
        selb_ref[g, n_s:, :] = jnp.full((selb_ref.shape[1] - n_s, QBLK), NEG, F32)

    per_chunk = ck // SLC_LEN
    krow = lax.broadcasted_iota(I32, (ck, 1), 0)

    def stream(g):
        def sel_bias(c):
            rows = [jnp.broadcast_to(selb_ref[g, pl.ds(c * per_chunk + i, 1), :], (SLC_LEN, QBLK))
                    for i in range(per_chunk)]
            return jnp.where(c * ck + krow <= tq, jnp.concatenate(rows, axis=0), NEG).astype(_MXU_DTYPE)

        def k_of(c):
            return ks_ref[0, pl.ds(pl.multiple_of(c * ck, ck), ck), lanes(g)]

        def v_of(c):
            return vs_ref[0, pl.ds(pl.multiple_of(c * ck, ck), ck), :]

        return (k_of, v_of, sel_bias, q_all[g], _init_state(cols, LANES), sa_ref.at[g], sb_ref.at[g])

    o_s = [_finish(st) for st in _attend_chunks([stream(g) for g in range(C_KV_GROUPS)], q0 // ck + 1, ck)]

    tiles = []
    for g, cg_ref in enumerate((cg0_ref, cg1_ref)):
        gate = _sigmoid(cg_ref[0].T[0:16])
        ys = []
        for r in range(hpg):
            c_ = slice(r * QBLK, (r + 1) * QBLK)
            rows = slice(g * HEAD_DIM, (g + 1) * HEAD_DIM)
            ys.append(gate[3 * r:3 * r + 1] * outs_c[g][:HEAD_DIM, c_] + gate[3 * r + 1:3 * r + 2] * o_s[g][rows, c_]
                      + gate[3 * r + 2:3 * r + 3] * o_w[g][rows, c_])
        tiles += [jnp.concatenate(ys[i:i + 2], axis=0).T for i in range(0, hpg, 2)]
    cz = cz_ref[0]
    o_ref[0] = (jnp.concatenate(tiles, axis=1) * (cz * _sigmoid(cz))).astype(o_ref.dtype)


def _nsa_mixer(u16, u32, kvcmp, overlap_t):
    b, s, _ = u16.shape
    ck = min(KCHUNK, s)
    gn = C_KV_GROUPS
    w = N_HEADS * HEAD_DIM
    ncp = kvcmp.shape[2]
    n_s = overlap_t.shape[0]
    kv5 = kvcmp.reshape(2, b, gn, ncp, LANES)
    cval = jnp.concatenate([kv5[1], jnp.broadcast_to(overlap_t.T[None, None], (b, gn, ncp, n_s))], axis=3)
    kw2 = gn * LANES
    v_spec = lambda name: pl.BlockSpec((1, s, LANES), lambda bi, qi: (bi, 0, _OFF16[name] // LANES))

    return pl.pallas_call(
        functools.partial(_nsa_kernel, ck=ck, win=WIN, n_sel=min(SLC_N, n_s)),
        grid=(b, s // QBLK),
        in_specs=[pl.BlockSpec((1, QBLK, w), lambda bi, qi: (bi, qi, _OFF16["c_q"] // w)),
                  pl.BlockSpec((1, 1, gn, ncp, LANES), lambda bi, qi: (0, bi, 0, 0, 0)),
                  pl.BlockSpec((1, gn, ncp, LANES + n_s), lambda bi, qi: (bi, 0, 0, 0)),
                  pl.BlockSpec((1, s, kw2), lambda bi, qi: (bi, 0, _OFF16["c_ks0"] // kw2)), v_spec("c_vs"),
                  pl.BlockSpec((1, s, kw2), lambda bi, qi: (bi, 0, _OFF16["c_kw0"] // kw2)), v_spec("c_vw"),
                  pl.BlockSpec((1, QBLK, LANES), lambda bi, qi: (bi, qi, _OFF32["c_g0"] // LANES)),
                  pl.BlockSpec((1, QBLK, LANES), lambda bi, qi: (bi, qi, _OFF32["c_g1"] // LANES)),
                  pl.BlockSpec((1, QBLK, w), lambda bi, qi: (bi, qi, _OFF32["c_z"] // w))],
        out_specs=pl.BlockSpec((1, QBLK, w), lambda bi, qi: (bi, qi, 0)),
        out_shape=jax.ShapeDtypeStruct((b, s, w), _MXU_DTYPE),
        scratch_shapes=[pltpu.VMEM((gn, n_s + ck // SLC_LEN, QBLK), F32),
                        pltpu.VMEM((gn, ck, HEADS_PER_GROUP * QBLK), F32),
                        pltpu.VMEM((gn, ck, HEADS_PER_GROUP * QBLK), F32)],
        compiler_params=pltpu.CompilerParams(dimension_semantics=("arbitrary", "arbitrary"),
                                             vmem_limit_bytes=VMEM_LIMIT),
        name="nsa_mixer",
    )(u16, kv5, cval, u16, u16, u16, u16, u32, u32, u32)


def _dsa_kernel(q_ref, iq_ref, k_ref, v_ref, ik_ref, iw_ref, bz_ref, o_ref, keys_ref, bias_ref, cut_ref,
                sa_ref, sb_ref, *, ck, topk, nbits):
    qi = pl.program_id(1)
    q0 = qi * QBLK
    nch = q0 // ck + 1
    tq = q0 + lax.broadcasted_iota(I32, (1, QBLK), 1)
    krow = lax.broadcasted_iota(I32, (ck, 1), 0)
    srow = lax.broadcasted_iota(I32, (8, 1), 0)

    iq_all = _heads_t(iq_ref[0].astype(F32), IDX_HEADS * IDX_DIM // LANES, IDX_DIM)
    iw_t = (iw_ref[0] * (IDX_HEADS ** -0.5 * IDX_DIM ** -0.5)).T

    def fill_rel(dst, c):
        cc = jnp.minimum(c, nch - 1)
        dst[...] = _dot(ik_ref[0, pl.ds(pl.multiple_of(cc * ck, ck), ck), :], iq_all)

    def keys_from(src, c):
        rel = src[...]
        sc = jnp.maximum(rel[:, 0:QBLK], 0.0) * iw_t[0:1]
        for h in range(1, IDX_HEADS):
            sc = sc + jnp.maximum(rel[:, h * QBLK:(h + 1) * QBLK], 0.0) * iw_t[h:h + 1]
        sc = jnp.where(c * ck + krow <= tq, sc, NEG)
        sc = jnp.where(sc == 0.0, 0.0, sc)
        bits = pltpu.bitcast(sc, I32)
        keys_ref[c] = bits ^ (lax.shift_right_arithmetic(bits, 31) & 0x7FFFFFFF)

    def score_body(j, _):
        c = 2 * j
        fill_rel(sb_ref, c + 1)
        keys_from(sa_ref, c)
        fill_rel(sa_ref, c + 2)
        keys_from(sb_ref, c + 1)
        return 0

    fill_rel(sa_ref, 0)
    lax.fori_loop(0, (nch + 1) // 2, score_body, 0)

    n_acc = 4

    def count(pred):
        def body(c, accs):
            accs = list(accs)
            for r in range(ck // 8):
                k = keys_ref[c, r * 8:(r + 1) * 8, :]
                accs[r % n_acc] = accs[r % n_acc] + jnp.where(pred(k, c * ck + r * 8 + srow), 1.0, 0.0)
            return tuple(accs)
        accs = lax.fori_loop(0, nch, body, tuple(jnp.zeros((8, LANES), F32) for _ in range(n_acc)))
        return jnp.sum(sum(accs[1:], accs[0]), axis=0, keepdims=True)

    def bit_body(i, thr):
        cand = thr + lax.shift_left(jnp.int32(1), 31 - i)
        return jnp.where(count(lambda k, _: k >= cand) >= topk, cand, thr)

    thr = lax.fori_loop(0, 32, bit_body, jnp.full((1, LANES), -2 ** 31, I32))
    c_gt = count(lambda k, _: k > thr)
    c_eq = count(lambda k, _: k >= thr) - c_gt
    need = topk - c_gt

    cut_ref[...] = jnp.full(cut_ref.shape, 2 ** nbits, I32)

    @pl.when(jnp.max(jnp.where(c_eq > need, 1.0, 0.0)) > 0.0)
    def _():
        def tie_body(i, cut):
            cand = cut + lax.shift_left(jnp.int32(1), nbits - 1 - i)
            below = count(lambda k, kpos: (k == thr) & (kpos < cand))
            return jnp.where(below < need, cand, cut)
        cut = lax.fori_loop(0, nbits, tie_body, jnp.zeros((1, LANES), I32))
        cut_ref[...] = jnp.broadcast_to(cut, cut_ref.shape)

    cut = cut_ref[0:1, :]

    def bias_body(c, _):
        k = keys_ref[jnp.minimum(c, nch - 1)]
        kpos = c * ck + krow
        chosen = (k > thr) | ((k == thr) & (kpos <= cut))
        bias_ref[c] = jnp.where(chosen & (kpos <= tq), 0.0, NEG).astype(bias_ref.dtype)
        return 0

    lax.fori_loop(0, 2 * ((nch + 1) // 2), bias_body, 0)

    q_all = _with_mask_rows(_heads_t(q_ref[0].astype(F32), N_HEADS * HEAD_DIM // LANES, HEAD_DIM))

    chunk = lambda ref: lambda c: ref[0, pl.ds(pl.multiple_of(c * ck, ck), ck), :]
    dsa_stream = (chunk(k_ref), chunk(v_ref), lambda c: bias_ref[c], q_all, _init_state(N_HEADS * QBLK, LANES),
                  sa_ref, sb_ref)
    out = _finish(_attend_chunks([dsa_stream], nch, ck)[0])[:HEAD_DIM]
    tiles = [jnp.concatenate([out[:, h * QBLK:(h + 1) * QBLK], out[:, (h + 1) * QBLK:(h + 2) * QBLK]], axis=0).T
             for h in range(0, N_HEADS, 2)]
    bz = bz_ref[0]
    o_ref[0] = (jnp.concatenate(tiles, axis=1) * (bz * _sigmoid(bz))).astype(o_ref.dtype)


def _dsa_mixer(u16, u32):
    b, s, _ = u16.shape
    ck = min(KCHUNK, s)
    w = N_HEADS * HEAD_DIM
    iqw = IDX_HEADS * IDX_DIM

    def k_spec(name):
        return pl.BlockSpec((1, s, LANES), lambda bi, qi: (bi, 0, _OFF16[name] // LANES))

    return pl.pallas_call(
        functools.partial(_dsa_kernel, ck=ck, topk=min(DSA_TOPK, s // 4), nbits=int(s).bit_length()),
        grid=(b, s // QBLK),
        in_specs=[pl.BlockSpec((1, QBLK, w), lambda bi, qi: (bi, qi, _OFF16["b_q"] // w)),
                  pl.BlockSpec((1, QBLK, iqw), lambda bi, qi: (bi, qi, _OFF16["i_q"] // iqw)),
                  k_spec("b_k"), k_spec("b_v"), k_spec("i_k"),
                  pl.BlockSpec((1, QBLK, LANES), lambda bi, qi: (bi, qi, _OFF32["i_w"] // LANES)),
                  pl.BlockSpec((1, QBLK, w), lambda bi, qi: (bi, qi, _OFF32["b_z"] // w))],
        out_specs=pl.BlockSpec((1, QBLK, w), lambda bi, qi: (bi, qi, 0)),
        out_shape=jax.ShapeDtypeStruct((b, s, w), _MXU_DTYPE),
        scratch_shapes=[pltpu.VMEM((s // ck + s // ck % 2, ck, QBLK), I32),
                        pltpu.VMEM((s // ck + s // ck % 2, ck, QBLK), _MXU_DTYPE),
                        pltpu.VMEM((8, LANES), I32),
                        pltpu.VMEM((ck, N_HEADS * QBLK), F32),
                        pltpu.VMEM((ck, N_HEADS * QBLK), F32)],
        compiler_params=pltpu.CompilerParams(dimension_semantics=("arbitrary", "arbitrary"),
                                             vmem_limit_bytes=VMEM_LIMIT),
        name="dsa_mixer",
    )(u16, u16, u16, u16, u16, u32, u32)


def _merge_kernel(x_ref, ya_ref, yb_ref, yc_ref, g0_ref, g1_ref, g2_ref, wa_ref, wb_ref, wc_ref, wo_ref,
                  lg_ref, lb_ref, o32_ref, o16_ref, *, alpha):
    m = (_sigmoid(g0_ref[...]) * _dot(ya_ref[...], wa_ref[...])
         + _sigmoid(g1_ref[...]) * _dot(yb_ref[...], wb_ref[...])
         + _sigmoid(g2_ref[...]) * _dot(yc_ref[...], wc_ref[...]))
    z = alpha * x_ref[...] + _dot(m.astype(_MXU_DTYPE), wo_ref[...])
    mu = jnp.mean(z, axis=1, keepdims=True)
    zc = z - mu
    var = jnp.mean(zc * zc, axis=1, keepdims=True)
    y = zc * lax.rsqrt(var + LN_EPS) * lg_ref[...] + lb_ref[...]
    o32_ref[...] = y
    o16_ref[...] = y.astype(o16_ref.dtype)


def _merge(x, ya, yb, yc, u32, wa, wb, wc, wo, lg, lb, alpha, tm=512):
    m, d = x.shape
    w = ya.shape[1]
    gbase = _OFF32["g_merge"] // d
    row = lambda i: (i, 0)
    const = lambda i: (0, 0)
    return pl.pallas_call(
        functools.partial(_merge_kernel, alpha=alpha),
        grid=(m // tm,),
        in_specs=[pl.BlockSpec((tm, d), row),
                  pl.BlockSpec((tm, w), row), pl.BlockSpec((tm, w), row), pl.BlockSpec((tm, w), row),
                  pl.BlockSpec((tm, d), lambda i: (i, gbase)),
                  pl.BlockSpec((tm, d), lambda i: (i, gbase + 1)),
                  pl.BlockSpec((tm, d), lambda i: (i, gbase + 2)),
                  pl.BlockSpec((w, d), const), pl.BlockSpec((w, d), const), pl.BlockSpec((w, d), const),
                  pl.BlockSpec((d, d), const), pl.BlockSpec((1, d), const), pl.BlockSpec((1, d), const)],
        out_specs=[pl.BlockSpec((tm, d), row), pl.BlockSpec((tm, d), row)],
        out_shape=[jax.ShapeDtypeStruct((m, d), F32), jax.ShapeDtypeStruct((m, d), _MXU_DTYPE)],
        compiler_params=pltpu.CompilerParams(dimension_semantics=("arbitrary",), vmem_limit_bytes=VMEM_LIMIT),
        name="merge_out_ln",
    )(x, ya, yb, yc, u32, u32, u32, wa, wb, wc, wo, lg, lb)


def _overlap_matrix(ncp, s):
    n_c = (s - CMP_LEN) // CMP_STRIDE + 1
    c_start = np.arange(ncp) * CMP_STRIDE
    s_start = np.arange(s // SLC_LEN) * SLC_LEN
    ov = (c_start[:, None] <= s_start[None, :] + SLC_LEN - 1) & (c_start[:, None] + CMP_LEN - 1 >= s_start[None, :])
    ov &= (np.arange(ncp) < n_c)[:, None]
    return ov.astype(np.float32)


def _layer(x32, x16, p, consts, alpha):
    b, s, d = x32.shape
    m = b * s
    x16 = x16.reshape(m, d)
    u32 = _matmul_bias(x16, p["w32"], p["b32"], F32, min(2048, m), _TN32, "in_proj_f32").reshape(b, s, _N32)
    u16 = _matmul_bias(x16, p["w16"], p["b16"], _MXU_DTYPE, min(2048, m), _TN16, "in_proj_bf16").reshape(b, s, _N16)

    y_a = _pool_mixer(u32, p["pool_w"], p["pool_b"], p["pool_scale"])

    nch16 = s // CMP_STRIDE
    tok = u32[:, :, _OFF32["c_kc"]:_OFF32["c_kc"] + 2 * LANES]
    chunks = (tok.reshape(b, nch16, CMP_STRIDE, 2, C_KV_GROUPS, HEAD_DIM)
              .transpose(3, 0, 4, 1, 2, 5).reshape(2, b * C_KV_GROUPS, nch16, CMP_STRIDE * HEAD_DIM))
    kvcmp = _compress(chunks, p["cmp_pos"], p["cmp_w1"], p["cmp_w2"])

    y_c = _nsa_mixer(u16, u32, kvcmp, consts["overlap_t"])
    y_b = _dsa_mixer(u16, u32)

    x32n, x16n = _merge(x32.reshape(m, d), y_a.reshape(m, -1), y_b.reshape(m, -1), y_c.reshape(m, -1),
                        u32.reshape(m, _N32), p["w_pa"], p["w_pb"], p["w_pc"], p["w_o"], p["ln_g"], p["ln_b"], alpha)
    return x32n.reshape(b, s, d), x16n.reshape(b, s, d)


def _prepare_params(w_in, b_in, pool_w, pool_b, pool_scale, cmp_pos_k, cmp_pos_v, cmp_w1_k, cmp_w2_k,
                    cmp_w1_v, cmp_w2_v, w_proj_a, w_proj_b, w_proj_c, w_o, ln_g, ln_b):
    nl = w_in.shape[0]
    mx = _MXU_DTYPE
    half = CMP_LEN // 2
    pos = jnp.stack([cmp_pos_k, cmp_pos_v], axis=1).reshape(nl, 2, 2, half * HEAD_DIM)
    return {
        "w32": (_gather_cols(w_in, _IDX32) * _SCALE32).astype(mx),
        "b32": (_gather_cols(b_in, _IDX32) * _SCALE32)[:, None, :],
        "w16": (_gather_cols(w_in, _IDX16) * _SCALE16).astype(mx),
        "b16": (_gather_cols(b_in, _IDX16) * _SCALE16)[:, None, :],
        "pool_w": pool_w.astype(mx),
        "pool_b": pool_b.reshape(nl, 1, -1),
        "pool_scale": pool_scale.reshape(nl, 1, -1),
        "cmp_pos": pos,
        "cmp_w1": jnp.stack([cmp_w1_k, cmp_w1_v], axis=1).astype(mx),
        "cmp_w2": jnp.concatenate([jnp.stack([cmp_w2_k, cmp_w2_v], axis=1)] * 2, axis=-1).astype(mx),
        "w_pa": w_proj_a.astype(mx), "w_pb": w_proj_b.astype(mx), "w_pc": w_proj_c.astype(mx),
        "w_o": w_o.astype(mx),
        "ln_g": ln_g[:, None, :], "ln_b": ln_b[:, None, :],
    }


def kernel(x, w_in, b_in, pool_w, pool_b, pool_scale, cmp_pos_k, cmp_pos_v, cmp_w1_k, cmp_w2_k, cmp_w1_v, cmp_w2_v, w_proj_a, w_proj_b, w_proj_c, w_o, ln_g, ln_b):
    depth = w_in.shape[0]
    s = x.shape[1]
    params = _prepare_params(w_in, b_in, pool_w, pool_b, pool_scale, cmp_pos_k, cmp_pos_v, cmp_w1_k, cmp_w2_k,
                             cmp_w1_v, cmp_w2_v, w_proj_a, w_proj_b, w_proj_c, w_o, ln_g, ln_b)
    consts = {"overlap_t": jnp.asarray(_overlap_matrix(s // CMP_STRIDE, s).T, _MXU_DTYPE)}
    alpha = (2 * depth) ** 0.25
    h32, h16 = x, x.astype(_MXU_DTYPE)
    for l in range(depth):
        h32, h16 = _layer(h32, h16, {k: v[l] for k, v in params.items()}, consts, alpha)
    return h32
```

```python
import functools

import numpy as np
import jax
import jax.numpy as jnp
from jax import lax
from jax.experimental import pallas as pl
from jax.experimental.pallas import tpu as pltpu

F32 = jnp.float32
I32 = jnp.int32
_MXU_DTYPE = jnp.bfloat16

D_MODEL = 1024
HEAD_DIM = 64
LANES = 128
POOL_WINDOWS = (2, 4, 8, 16)
POOL_GC = 128
N_HEADS = 8
IDX_HEADS = 8
IDX_DIM = 32
DSA_TOPK = 256
C_KV_GROUPS = 2
HEADS_PER_GROUP = N_HEADS // C_KV_GROUPS
CMP_LEN = 32
CMP_STRIDE = 16
SLC_LEN = 64
SLC_N = 16
WIN = 512
FORCE_BONUS = 1e4
LN_EPS = 1e-5
NEG = -1e30
QBLK = 128
KCHUNK = 512
VMEM_LIMIT = 56 * 1024 * 1024

_IN_WIDTHS = (512, 512, 512, 64, 64, 512, 256, 32, 8, 512, 128, 128, 128, 128, 128, 128, 24, 512, 3072)
_IN_NAMES = ("a_x", "a_z", "b_q", "b_k", "b_v", "b_z", "i_q", "i_k", "i_w", "c_q", "c_kc", "c_vc",
             "c_ks", "c_vs", "c_kw", "c_vw", "c_g", "c_z", "g_merge")
_N_IN = sum(_IN_WIDTHS)
_OFF = dict(zip(_IN_NAMES, np.cumsum((0,) + _IN_WIDTHS[:-1])))
_WID = dict(zip(_IN_NAMES, _IN_WIDTHS))


def _seg(name, lo=0, hi=None):
    hi = _WID[name] if hi is None else hi
    return np.arange(_OFF[name] + lo, _OFF[name] + hi)


def _pad(n):
    return np.full((n,), _N_IN)


def _layout32():
    segs, off, pos = [], {}, 0

    def add(name, idx):
        nonlocal pos
        off[name] = pos
        segs.append(idx)
        pos += len(idx)

    add("a_x", _seg("a_x"))
    add("a_z", _seg("a_z"))
    add("b_z", _seg("b_z"))
    add("c_z", _seg("c_z"))
    add("g_merge", _seg("g_merge"))
    add("c_kc", _seg("c_kc"))
    add("c_vc", _seg("c_vc"))
    add("i_w", np.concatenate([_seg("i_w"), _pad(LANES - 8)]))
    for g in range(C_KV_GROUPS):
        add(f"c_g{g}", np.concatenate([_seg("c_g", 12 * g, 12 * g + 12), _pad(LANES - 12)]))
    idx = np.concatenate(segs)
    return idx, np.ones((len(idx),), np.float32), off


def _layout16():
    segs, scales, off, pos = [], [], {}, 0

    def add(name, idx, scale=1.0):
        nonlocal pos
        off[name] = pos
        segs.append(idx)
        scales.append(np.full((len(idx),), scale, np.float32))
        pos += len(idx)

    qk_scale = HEAD_DIM ** -0.5 * float(np.log2(np.e))
    add("b_q", _seg("b_q"), qk_scale)
    add("c_q", _seg("c_q"), qk_scale)
    add("i_q", _seg("i_q"))
    for name in ("c_ks", "c_kw"):
        for g in range(C_KV_GROUPS):
            one = _seg(name, HEAD_DIM * g, HEAD_DIM * (g + 1))
            add(f"{name}{g}", np.concatenate([one, one]))
    add("b_k", np.concatenate([_seg("b_k")] * 2))
    add("i_k", np.concatenate([_seg("i_k")] * (LANES // IDX_DIM)))
    add("c_vs", _seg("c_vs"))
    add("c_vw", _seg("c_vw"))
    add("b_v", np.concatenate([_seg("b_v"), _pad(LANES - HEAD_DIM)]))
    add("pad", _pad(LANES))
    return np.concatenate(segs), np.concatenate(scales), off


def _gather_cols(a, idx):
    pieces, i = [], 0
    while i < len(idx):
        j = i + 1
        if idx[i] == _N_IN:
            while j < len(idx) and idx[j] == _N_IN:
                j += 1
            pieces.append(jnp.zeros(a.shape[:-1] + (j - i,), a.dtype))
        else:
            while j < len(idx) and idx[j] == idx[j - 1] + 1:
                j += 1
            pieces.append(a[..., int(idx[i]):int(idx[i]) + (j - i)])
        i = j
    return jnp.concatenate(pieces, axis=-1)


_IDX32, _SCALE32, _OFF32 = _layout32()
_IDX16, _SCALE16, _OFF16 = _layout16()
_N32 = len(_IDX32)
_N16 = len(_IDX16)
_TN32 = 640
_TN16 = 512
assert _N32 % _TN32 == 0 and _N16 % _TN16 == 0


def _sigmoid(x):
    return 1.0 / (1.0 + jnp.exp(-x))


def _dot(a, b):
    return jnp.dot(a, b, preferred_element_type=F32)


def _mm_bias_kernel(x_ref, w_ref, b_ref, o_ref):
    o_ref[...] = (_dot(x_ref[...], w_ref[...]) + b_ref[...]).astype(o_ref.dtype)


def _matmul_bias(x, w, b, out_dtype, tm, tn, name):
    m, k = x.shape
    n = w.shape[1]
    return pl.pallas_call(
        _mm_bias_kernel,
        grid=(m // tm, n // tn),
        in_specs=[pl.BlockSpec((tm, k), lambda i, j: (i, 0)),
                  pl.BlockSpec((k, tn), lambda i, j: (0, j)),
                  pl.BlockSpec((1, tn), lambda i, j: (0, j))],
        out_specs=pl.BlockSpec((tm, tn), lambda i, j: (i, j)),
        out_shape=jax.ShapeDtypeStruct((m, n), out_dtype),
        compiler_params=pltpu.CompilerParams(dimension_semantics=("arbitrary", "arbitrary"),
                                             vmem_limit_bytes=VMEM_LIMIT),
        name=name,
    )(x, w, b)


_HALO = 16


def _pool_kernel(xa_ref, halo_ref, az_ref, pw_ref, pb_ref, ps_ref, o_ref, *, tb):
    i = pl.program_id(1)
    cur = xa_ref[0]
    halo = jnp.where(i > 0, halo_ref[0], 0.0)
    ext = jnp.concatenate([halo, cur], axis=0)
    pos = (i * tb + 1 + lax.broadcasted_iota(I32, (tb, 1), 0)).astype(F32)
    outs = []
    for g, wnd in enumerate(POOL_WINDOWS):
        s = ext[:, g * POOL_GC:(g + 1) * POOL_GC]
        k = 1
        while k < wnd:
            s = s + pltpu.roll(s, k, axis=0)
            k *= 2
        mean = s[_HALO:] / jnp.minimum(pos, float(wnd))
        pooled = mean - cur[:, g * POOL_GC:(g + 1) * POOL_GC]
        outs.append(_dot(pooled.astype(_MXU_DTYPE), pw_ref[g]))
    y = jnp.concatenate(outs, axis=1) + pb_ref[...]
    az = az_ref[0]
    o_ref[0] = (y * ps_ref[...] * (az * _sigmoid(az))).astype(o_ref.dtype)


def _pool_mixer(u32, pool_w, pool_b, pool_scale, tb=512):
    b, s, _ = u32.shape
    hb = tb // _HALO
    return pl.pallas_call(
        functools.partial(_pool_kernel, tb=tb),
        grid=(b, s // tb),
        in_specs=[pl.BlockSpec((1, tb, 512), lambda bi, i: (bi, i, _OFF32["a_x"] // 512)),
                  pl.BlockSpec((1, _HALO, 512), lambda bi, i: (bi, jnp.maximum(i * hb - 1, 0), _OFF32["a_x"] // 512)),
                  pl.BlockSpec((1, tb, 512), lambda bi, i: (bi, i, _OFF32["a_z"] // 512)),
                  pl.BlockSpec((4, POOL_GC, POOL_GC), lambda bi, i: (0, 0, 0)),
                  pl.BlockSpec((1, 512), lambda bi, i: (0, 0)),
                  pl.BlockSpec((1, 512), lambda bi, i: (0, 0))],
        out_specs=pl.BlockSpec((1, tb, 512), lambda bi, i: (bi, i, 0)),
        out_shape=jax.ShapeDtypeStruct((b, s, 512), _MXU_DTYPE),
        compiler_params=pltpu.CompilerParams(dimension_semantics=("arbitrary", "arbitrary"),
                                             vmem_limit_bytes=VMEM_LIMIT),
        name="pool_mixer",
    )(u32, u32, u32, pool_w, pool_b, pool_scale)


def _compress_kernel(x_ref, pos_ref, w1_ref, w2_ref, o_ref):
    n = x_ref.shape[1] // CMP_STRIDE
    a = bb = None
    for l in range(CMP_STRIDE):
        x_l = x_ref[0, pl.ds(l, n, stride=CMP_STRIDE), :]
        lo = _dot((x_l + pos_ref[0, l:l + 1, :]).astype(_MXU_DTYPE), w1_ref[0, l])
        hi = _dot((x_l + pos_ref[0, CMP_STRIDE + l:CMP_STRIDE + l + 1, :]).astype(_MXU_DTYPE),
                  w1_ref[0, CMP_STRIDE + l])
        a = lo if a is None else a + lo
        bb = hi if bb is None else bb + hi
    h = a + pltpu.roll(bb, n - 1, axis=0)
    h = h * _sigmoid(h)
    o_ref[0, 0] = _dot(h.astype(_MXU_DTYPE), w2_ref[0]).astype(o_ref.dtype)


def _compress(u32, pos, w1, w2):
    b, s, _ = u32.shape
    n = s // CMP_STRIDE
    base = _OFF32["c_kc"] // LANES
    return pl.pallas_call(
        _compress_kernel,
        grid=(2, b),
        in_specs=[pl.BlockSpec((1, s, LANES), lambda kv, bi: (bi, 0, base + kv)),
                  pl.BlockSpec((1, CMP_LEN, LANES), lambda kv, bi: (kv, 0, 0)),
                  pl.BlockSpec((1, CMP_LEN, LANES, LANES), lambda kv, bi: (kv, 0, 0, 0)),
                  pl.BlockSpec((1, LANES, C_KV_GROUPS * LANES), lambda kv, bi: (kv, 0, 0))],
        out_specs=pl.BlockSpec((1, 1, n, C_KV_GROUPS * LANES), lambda kv, bi: (kv, bi, 0, 0)),
        out_shape=jax.ShapeDtypeStruct((2, b, n, C_KV_GROUPS * LANES), _MXU_DTYPE),
        compiler_params=pltpu.CompilerParams(dimension_semantics=("arbitrary", "arbitrary"),
                                             vmem_limit_bytes=VMEM_LIMIT),
        name="nsa_compress",
    )(u32, pos, w1, w2)


def _heads_t(qf, n_tiles, width):
    lane = lax.broadcasted_iota(I32, (1, LANES), 1)
    per_tile = LANES // width
    out = []
    for tix in range(n_tiles):
        qt = qf[:, tix * LANES:(tix + 1) * LANES]
        for j in range(per_tile):
            keep = (lane >= j * width) & (lane < (j + 1) * width)
            out.append(jnp.where(keep, qt, 0.0).T.astype(_MXU_DTYPE))
    return jnp.concatenate(out, axis=1)


def _init_state(cols, rows):
    return (jnp.full((1, cols), -1e38, F32), jnp.zeros((1, cols), F32), jnp.zeros((rows, cols), F32))


def _with_mask_rows(q_all):
    cols = q_all.shape[1]
    r = lax.broadcasted_iota(I32, (LANES, cols), 0)
    c = lax.broadcasted_iota(I32, (LANES, cols), 1)
    eye = jnp.where((c & (LANES - 1)) == r, 1.0, 0.0).astype(_MXU_DTYPE)
    return jnp.concatenate([q_all, eye], axis=0)


def _update(s, vt_blk, state):
    m, l, acc = state
    m_new = jnp.maximum(m, jnp.max(s, axis=0, keepdims=True))
    a = jnp.exp2(m - m_new)
    p = jnp.exp2(s - m_new)
    l = a * l + jnp.sum(p, axis=0, keepdims=True)
    acc = a * acc + lax.dot_general(vt_blk, p.astype(_MXU_DTYPE), (((0,), (0,)), ((), ())),
                                    preferred_element_type=F32)
    return m_new, l, acc


def _scores(k_blk, bias, q_aug):
    return _dot(jnp.concatenate([k_blk, bias], axis=1), q_aug)


def _attend(k_blk, vt_blk, q_aug, bias, state):
    return _update(_scores(k_blk, bias, q_aug), vt_blk, state)


def _attend_chunks(streams, nch, ck):
    last = nch - 1
    trips = (nch + 1) // 2

    def fill(st, dst, c):
        k_of, _, bias_of, q_aug = st[:4]
        dst[...] = _scores(k_of(jnp.minimum(c, last)), bias_of(jnp.minimum(c, 2 * trips - 1)), q_aug)

    def body(j, states):
        c = 2 * j
        for st in streams:
            fill(st, st[6], c + 1)
        states = [_update(st[5][...], st[1](c), state) for st, state in zip(streams, states)]
        for st in streams:
            fill(st, st[5], c + 2)
        return tuple(_update(st[6][...], st[1](jnp.minimum(c + 1, last)), state)
                     for st, state in zip(streams, states))

    for st in streams:
        fill(st, st[5], 0)
    return lax.fori_loop(0, trips, body, tuple(st[4] for st in streams))


def _finish(state):
    _, l, acc = state
    return acc / l


def _nsa_kernel(q_ref, kc_ref, cv_ref, ks_ref, vs_ref, kw_ref, vw_ref, cg0_ref, cg1_ref, cz_ref, o_ref, selb_ref,
                sa_ref, sb_ref, *, ck, win, n_sel):
    qi = pl.program_id(1)
    q0 = qi * QBLK
    hpg = HEADS_PER_GROUP
    cols = hpg * QBLK
    gw = hpg * HEAD_DIM
    tq = q0 + lax.broadcasted_iota(I32, (1, QBLK), 1)
    ncp = kc_ref.shape[2]
    n_s = cv_ref.shape[3] - LANES
    qf = q_ref[0].astype(F32)
    q_all = [_with_mask_rows(_heads_t(qf[:, g * gw:(g + 1) * gw], hpg // 2, HEAD_DIM))
             for g in range(C_KV_GROUPS)]
    lanes = lambda g: slice(g * LANES, (g + 1) * LANES)

    cend = lax.broadcasted_iota(I32, (ncp, 1), 0) * CMP_STRIDE + (CMP_LEN - 1)
    bias_c = jnp.where(cend <= tq, 0.0, NEG).astype(_MXU_DTYPE)
    seen = jnp.concatenate([tq >= CMP_LEN - 1] * hpg, axis=1)
    outs_c = [jnp.where(seen, _finish(_attend(kc_ref[0, 0, :, lanes(g)], cv_ref[0, g], q_all[g], bias_c,
                                              _init_state(cols, LANES + n_s))), 0.0)
              for g in range(C_KV_GROUPS)]

    span = win + QBLK
    start = pl.multiple_of(jnp.maximum(q0 - win, 0), QBLK)
    kpos = start + lax.broadcasted_iota(I32, (span, 1), 0)
    bias_w = jnp.where((kpos <= tq) & (kpos > tq - win), 0.0, NEG).astype(_MXU_DTYPE)
    o_w = []
    for g in range(C_KV_GROUPS):
        o_w.append(_finish(_attend(kw_ref[0, pl.ds(start, span), lanes(g)], vw_ref[0, pl.ds(start, span), :],
                                   q_all[g], bias_w, _init_state(cols, LANES))))

    ji = lax.broadcasted_iota(I32, (n_s, 1), 0)
    jf = ji.astype(F32)
    blk = lax.shift_right_logical(tq, 6)
    forced = (ji == 0) | (ji == blk) | (ji == blk - 1)
    for g in range(C_KV_GROUPS):
        out = outs_c[g]
        work = out[LANES:, 0:QBLK]
        for r in range(1, hpg):
            work = work + out[LANES:, r * QBLK:(r + 1) * QBLK]
        work = work + jnp.where(forced, FORCE_BONUS, 0.0)
        work = jnp.where(ji * SLC_LEN <= tq, work, NEG)
        sel = jnp.zeros((n_s, QBLK), F32)
        for _ in range(n_sel):
            mx = jnp.max(work, axis=0, keepdims=True)
            first = jnp.min(jnp.where(work == mx, jf, float(n_s)), axis=0, keepdims=True)
            hit = jf == first
            sel = jnp.where(hit, 1.0, sel)
            work = jnp.where(hit, -3e38, work)
        selb_ref[g, 0:n_s, :] = jnp.where(sel > 0.5, 0.0, NEG)
        selb_ref[g, n_s:, :] = jnp.full((selb_ref.shape[1] - n_s, QBLK), NEG, F32)

    per_chunk = ck // SLC_LEN
    krow = lax.broadcasted_iota(I32, (ck, 1), 0)

    def stream(g):
        def sel_bias(c):
            rows = [jnp.broadcast_to(selb_ref[g, pl.ds(c * per_chunk + i, 1), :], (SLC_LEN, QBLK))
                    for i in range(per_chunk)]
            return jnp.where(c * ck + krow <= tq, jnp.concatenate(rows, axis=0), NEG).astype(_MXU_DTYPE)

        def k_of(c):
            return ks_ref[0, pl.ds(pl.multiple_of(c * ck, ck), ck), lanes(g)]

        def v_of(c):
            return vs_ref[0, pl.ds(pl.multiple_of(c * ck, ck), ck), :]

        return (k_of, v_of, sel_bias, q_all[g], _init_state(cols, LANES), sa_ref.at[g], sb_ref.at[g])

    o_s = [_finish(st) for st in _attend_chunks([stream(g) for g in range(C_KV_GROUPS)], q0 // ck + 1, ck)]

    tiles = []
    for g, cg_ref in enumerate((cg0_ref, cg1_ref)):
        gate = _sigmoid(cg_ref[0].T[0:16])
        ys = []
        for r in range(hpg):
            c_ = slice(r * QBLK, (r + 1) * QBLK)
            rows = slice(g * HEAD_DIM, (g + 1) * HEAD_DIM)
            ys.append(gate[3 * r:3 * r + 1] * outs_c[g][:HEAD_DIM, c_] + gate[3 * r + 1:3 * r + 2] * o_s[g][rows, c_]
                      + gate[3 * r + 2:3 * r + 3] * o_w[g][rows, c_])
        tiles += [jnp.concatenate(ys[i:i + 2], axis=0).T for i in range(0, hpg, 2)]
    cz = cz_ref[0]
    o_ref[0] = (jnp.concatenate(tiles, axis=1) * (cz * _sigmoid(cz))).astype(o_ref.dtype)


def _nsa_mixer(u16, u32, kvcmp, overlap_t):
    b, s, _ = u16.shape
    ck = min(KCHUNK, s)
    gn = C_KV_GROUPS
    w = N_HEADS * HEAD_DIM
    ncp = kvcmp.shape[2]
    n_s = overlap_t.shape[0]
    vcmp = kvcmp[1].reshape(b, ncp, gn, LANES).transpose(0, 2, 1, 3)
    cval = jnp.concatenate([vcmp, jnp.broadcast_to(overlap_t.T[None, None], (b, gn, ncp, n_s))], axis=3)
    kw2 = gn * LANES
    v_spec = lambda name: pl.BlockSpec((1, s, LANES), lambda bi, qi: (bi, 0, _OFF16[name] // LANES))

    return pl.pallas_call(
        functools.partial(_nsa_kernel, ck=ck, win=WIN, n_sel=min(SLC_N, n_s)),
        grid=(b, s // QBLK),
        in_specs=[pl.BlockSpec((1, QBLK, w), lambda bi, qi: (bi, qi, _OFF16["c_q"] // w)),
                  pl.BlockSpec((1, 1, ncp, gn * LANES), lambda bi, qi: (0, bi, 0, 0)),
                  pl.BlockSpec((1, gn, ncp, LANES + n_s), lambda bi, qi: (bi, 0, 0, 0)),
                  pl.BlockSpec((1, s, kw2), lambda bi, qi: (bi, 0, _OFF16["c_ks0"] // kw2)), v_spec("c_vs"),
                  pl.BlockSpec((1, s, kw2), lambda bi, qi: (bi, 0, _OFF16["c_kw0"] // kw2)), v_spec("c_vw"),
                  pl.BlockSpec((1, QBLK, LANES), lambda bi, qi: (bi, qi, _OFF32["c_g0"] // LANES)),
                  pl.BlockSpec((1, QBLK, LANES), lambda bi, qi: (bi, qi, _OFF32["c_g1"] // LANES)),
                  pl.BlockSpec((1, QBLK, w), lambda bi, qi: (bi, qi, _OFF32["c_z"] // w))],
        out_specs=pl.BlockSpec((1, QBLK, w), lambda bi, qi: (bi, qi, 0)),
        out_shape=jax.ShapeDtypeStruct((b, s, w), _MXU_DTYPE),
        scratch_shapes=[pltpu.VMEM((gn, n_s + ck // SLC_LEN, QBLK), F32),
                        pltpu.VMEM((gn, ck, HEADS_PER_GROUP * QBLK), F32),
                        pltpu.VMEM((gn, ck, HEADS_PER_GROUP * QBLK), F32)],
        compiler_params=pltpu.CompilerParams(dimension_semantics=("arbitrary", "arbitrary"),
                                             vmem_limit_bytes=VMEM_LIMIT),
        name="nsa_mixer",
    )(u16, kvcmp, cval, u16, u16, u16, u16, u32, u32, u32)


def _dsa_kernel(q_ref, iq_ref, k_ref, v_ref, ik_ref, iw_ref, bz_ref, o_ref, keys_ref, bias_ref, cut_ref,
                sa_ref, sb_ref, *, ck, topk, nbits):
    qi = pl.program_id(1)
    q0 = qi * QBLK
    nch = q0 // ck + 1
    tq = q0 + lax.broadcasted_iota(I32, (1, QBLK), 1)
    krow = lax.broadcasted_iota(I32, (ck, 1), 0)
    srow = lax.broadcasted_iota(I32, (8, 1), 0)

    iq_all = _heads_t(iq_ref[0].astype(F32), IDX_HEADS * IDX_DIM // LANES, IDX_DIM)
    iw_t = (iw_ref[0] * (IDX_HEADS ** -0.5 * IDX_DIM ** -0.5)).T

    def fill_rel(dst, c):
        cc = jnp.minimum(c, nch - 1)
        dst[...] = _dot(ik_ref[0, pl.ds(pl.multiple_of(cc * ck, ck), ck), :], iq_all)

    def keys_from(src, c):
        rel = src[...]
        sc = jnp.maximum(rel[:, 0:QBLK], 0.0) * iw_t[0:1]
        for h in range(1, IDX_HEADS):
            sc = sc + jnp.maximum(rel[:, h * QBLK:(h + 1) * QBLK], 0.0) * iw_t[h:h + 1]
        sc = jnp.where(c * ck + krow <= tq, sc, NEG)
        sc = jnp.where(sc == 0.0, 0.0, sc)
        bits = pltpu.bitcast(sc, I32)
        keys_ref[c] = bits ^ (lax.shift_right_arithmetic(bits, 31) & 0x7FFFFFFF)

    def score_body(j, _):
        c = 2 * j
        fill_rel(sb_ref, c + 1)
        keys_from(sa_ref, c)
        fill_rel(sa_ref, c + 2)
        keys_from(sb_ref, c + 1)
        return 0

    fill_rel(sa_ref, 0)
    lax.fori_loop(0, (nch + 1) // 2, score_body, 0)

    n_acc = 4

    def count(pred):
        def body(c, accs):
            accs = list(accs)
            for r in range(ck // 8):
                k = keys_ref[c, r * 8:(r + 1) * 8, :]
                accs[r % n_acc] = accs[r % n_acc] + jnp.where(pred(k, c * ck + r * 8 + srow), 1.0, 0.0)
            return tuple(accs)
        accs = lax.fori_loop(0, nch, body, tuple(jnp.zeros((8, LANES), F32) for _ in range(n_acc)))
        return jnp.sum(sum(accs[1:], accs[0]), axis=0, keepdims=True)

    def bit_body(i, thr):
        cand = thr + lax.shift_left(jnp.int32(1), 31 - i)
        return jnp.where(count(lambda k, _: k >= cand) >= topk, cand, thr)

    thr = lax.fori_loop(0, 32, bit_body, jnp.full((1, LANES), -2 ** 31, I32))
    c_gt = count(lambda k, _: k > thr)
    c_eq = count(lambda k, _: k >= thr) - c_gt
    need = topk - c_gt

    cut_ref[...] = jnp.full(cut_ref.shape, 2 ** nbits, I32)

    @pl.when(jnp.max(jnp.where(c_eq > need, 1.0, 0.0)) > 0.0)
    def _():
        def tie_body(i, cut):
            cand = cut + lax.shift_left(jnp.int32(1), nbits - 1 - i)
            below = count(lambda k, kpos: (k == thr) & (kpos < cand))
            return jnp.where(below < need, cand, cut)
        cut = lax.fori_loop(0, nbits, tie_body, jnp.zeros((1, LANES), I32))
        cut_ref[...] = jnp.broadcast_to(cut, cut_ref.shape)

    cut = cut_ref[0:1, :]

    def bias_body(c, _):
        k = keys_ref[jnp.minimum(c, nch - 1)]
        kpos = c * ck + krow
        chosen = (k > thr) | ((k == thr) & (kpos <= cut))
        bias_ref[c] = jnp.where(chosen & (kpos <= tq), 0.0, NEG).astype(bias_ref.dtype)
        return 0

    lax.fori_loop(0, 2 * ((nch + 1) // 2), bias_body, 0)

    q_all = _with_mask_rows(_heads_t(q_ref[0].astype(F32), N_HEADS * HEAD_DIM // LANES, HEAD_DIM))

    chunk = lambda ref: lambda c: ref[0, pl.ds(pl.multiple_of(c * ck, ck), ck), :]
    dsa_stream = (chunk(k_ref), chunk(v_ref), lambda c: bias_ref[c], q_all, _init_state(N_HEADS * QBLK, LANES),
                  sa_ref, sb_ref)
    out = _finish(_attend_chunks([dsa_stream], nch, ck)[0])[:HEAD_DIM]
    tiles = [jnp.concatenate([out[:, h * QBLK:(h + 1) * QBLK], out[:, (h + 1) * QBLK:(h + 2) * QBLK]], axis=0).T
             for h in range(0, N_HEADS, 2)]
    bz = bz_ref[0]
    o_ref[0] = (jnp.concatenate(tiles, axis=1) * (bz * _sigmoid(bz))).astype(o_ref.dtype)


def _dsa_mixer(u16, u32):
    b, s, _ = u16.shape
    ck = min(KCHUNK, s)
    w = N_HEADS * HEAD_DIM
    iqw = IDX_HEADS * IDX_DIM

    def k_spec(name):
        return pl.BlockSpec((1, s, LANES), lambda bi, qi: (bi, 0, _OFF16[name] // LANES))

    return pl.pallas_call(
        functools.partial(_dsa_kernel, ck=ck, topk=min(DSA_TOPK, s // 4), nbits=int(s).bit_length()),
        grid=(b, s // QBLK),
        in_specs=[pl.BlockSpec((1, QBLK, w), lambda bi, qi: (bi, qi, _OFF16["b_q"] // w)),
                  pl.BlockSpec((1, QBLK, iqw), lambda bi, qi: (bi, qi, _OFF16["i_q"] // iqw)),
                  k_spec("b_k"), k_spec("b_v"), k_spec("i_k"),
                  pl.BlockSpec((1, QBLK, LANES), lambda bi, qi: (bi, qi, _OFF32["i_w"] // LANES)),
                  pl.BlockSpec((1, QBLK, w), lambda bi, qi: (bi, qi, _OFF32["b_z"] // w))],
        out_specs=pl.BlockSpec((1, QBLK, w), lambda bi, qi: (bi, qi, 0)),
        out_shape=jax.ShapeDtypeStruct((b, s, w), _MXU_DTYPE),
        scratch_shapes=[pltpu.VMEM((s // ck + s // ck % 2, ck, QBLK), I32),
                        pltpu.VMEM((s // ck + s // ck % 2, ck, QBLK), _MXU_DTYPE),
                        pltpu.VMEM((8, LANES), I32),
                        pltpu.VMEM((ck, N_HEADS * QBLK), F32),
                        pltpu.VMEM((ck, N_HEADS * QBLK), F32)],
        compiler_params=pltpu.CompilerParams(dimension_semantics=("arbitrary", "arbitrary"),
                                             vmem_limit_bytes=VMEM_LIMIT),
        name="dsa_mixer",
    )(u16, u16, u16, u16, u16, u32, u32)


def _merge_kernel(x_ref, ya_ref, yb_ref, yc_ref, g0_ref, g1_ref, g2_ref, wa_ref, wb_ref, wc_ref, wo_ref,
                  lg_ref, lb_ref, o32_ref, o16_ref, *, alpha):
    m = (_sigmoid(g0_ref[...]) * _dot(ya_ref[...], wa_ref[...])
         + _sigmoid(g1_ref[...]) * _dot(yb_ref[...], wb_ref[...])
         + _sigmoid(g2_ref[...]) * _dot(yc_ref[...], wc_ref[...]))
    z = alpha * x_ref[...] + _dot(m.astype(_MXU_DTYPE), wo_ref[...])
    mu = jnp.mean(z, axis=1, keepdims=True)
    zc = z - mu
    var = jnp.mean(zc * zc, axis=1, keepdims=True)
    y = zc * lax.rsqrt(var + LN_EPS) * lg_ref[...] + lb_ref[...]
    o32_ref[...] = y
    o16_ref[...] = y.astype(o16_ref.dtype)


def _merge(x, ya, yb, yc, u32, wa, wb, wc, wo, lg, lb, alpha, tm=512):
    m, d = x.shape
    w = ya.shape[1]
    gbase = _OFF32["g_merge"] // d
    row = lambda i: (i, 0)
    const = lambda i: (0, 0)
    return pl.pallas_call(
        functools.partial(_merge_kernel, alpha=alpha),
        grid=(m // tm,),
        in_specs=[pl.BlockSpec((tm, d), row),
                  pl.BlockSpec((tm, w), row), pl.BlockSpec((tm, w), row), pl.BlockSpec((tm, w), row),
                  pl.BlockSpec((tm, d), lambda i: (i, gbase)),
                  pl.BlockSpec((tm, d), lambda i: (i, gbase + 1)),
                  pl.BlockSpec((tm, d), lambda i: (i, gbase + 2)),
                  pl.BlockSpec((w, d), const), pl.BlockSpec((w, d), const), pl.BlockSpec((w, d), const),
                  pl.BlockSpec((d, d), const), pl.BlockSpec((1, d), const), pl.BlockSpec((1, d), const)],
        out_specs=[pl.BlockSpec((tm, d), row), pl.BlockSpec((tm, d), row)],
        out_shape=[jax.ShapeDtypeStruct((m, d), F32), jax.ShapeDtypeStruct((m, d), _MXU_DTYPE)],
        compiler_params=pltpu.CompilerParams(dimension_semantics=("arbitrary",), vmem_limit_bytes=VMEM_LIMIT),
        name="merge_out_ln",
    )(x, ya, yb, yc, u32, u32, u32, wa, wb, wc, wo, lg, lb)


def _overlap_matrix(ncp, s):
    n_c = (s - CMP_LEN) // CMP_STRIDE + 1
    c_start = np.arange(ncp) * CMP_STRIDE
    s_start = np.arange(s // SLC_LEN) * SLC_LEN
    ov = (c_start[:, None] <= s_start[None, :] + SLC_LEN - 1) & (c_start[:, None] + CMP_LEN - 1 >= s_start[None, :])
    ov &= (np.arange(ncp) < n_c)[:, None]
    return ov.astype(np.float32)


def _layer(x32, x16, p, consts, alpha):
    b, s, d = x32.shape
    m = b * s
    x16 = x16.reshape(m, d)
    u32 = _matmul_bias(x16, p["w32"], p["b32"], F32, min(2048, m), _TN32, "in_proj_f32").reshape(b, s, _N32)
    u16 = _matmul_bias(x16, p["w16"], p["b16"], _MXU_DTYPE, min(2048, m), _TN16, "in_proj_bf16").reshape(b, s, _N16)

    y_a = _pool_mixer(u32, p["pool_w"], p["pool_b"], p["pool_scale"])

    kvcmp = _compress(u32, p["cmp_pos"], p["cmp_w1"], p["cmp_w2"])

    y_c = _nsa_mixer(u16, u32, kvcmp, consts["overlap_t"])
    y_b = _dsa_mixer(u16, u32)

    x32n, x16n = _merge(x32.reshape(m, d), y_a.reshape(m, -1), y_b.reshape(m, -1), y_c.reshape(m, -1),
                        u32.reshape(m, _N32), p["w_pa"], p["w_pb"], p["w_pc"], p["w_o"], p["ln_g"], p["ln_b"], alpha)
    return x32n.reshape(b, s, d), x16n.reshape(b, s, d)


def _prepare_params(w_in, b_in, pool_w, pool_b, pool_scale, cmp_pos_k, cmp_pos_v, cmp_w1_k, cmp_w2_k,
                    cmp_w1_v, cmp_w2_v, w_proj_a, w_proj_b, w_proj_c, w_o, ln_g, ln_b):
    nl = w_in.shape[0]
    mx = _MXU_DTYPE
    gn = C_KV_GROUPS
    eye = jnp.eye(gn, dtype=F32)
    pos = jnp.stack([cmp_pos_k, cmp_pos_v], axis=1)
    w1 = jnp.stack([cmp_w1_k, cmp_w1_v], axis=1).reshape(nl, 2, CMP_LEN, HEAD_DIM, HEAD_DIM)
    w2 = jnp.stack([cmp_w2_k, cmp_w2_v], axis=1)
    return {
        "w32": (_gather_cols(w_in, _IDX32) * _SCALE32).astype(mx),
        "b32": (_gather_cols(b_in, _IDX32) * _SCALE32)[:, None, :],
        "w16": (_gather_cols(w_in, _IDX16) * _SCALE16).astype(mx),
        "b16": (_gather_cols(b_in, _IDX16) * _SCALE16)[:, None, :],
        "pool_w": pool_w.astype(mx),
        "pool_b": pool_b.reshape(nl, 1, -1),
        "pool_scale": pool_scale.reshape(nl, 1, -1),
        "cmp_pos": jnp.concatenate([pos] * gn, axis=-1),
        "cmp_w1": jnp.einsum("gh,nklij->nklgihj", eye, w1).reshape(nl, 2, CMP_LEN, LANES, LANES).astype(mx),
        "cmp_w2": jnp.einsum("gh,nkij,r->nkgihrj", eye, w2, jnp.ones((2,), F32))
                  .reshape(nl, 2, LANES, gn * LANES).astype(mx),
        "w_pa": w_proj_a.astype(mx), "w_pb": w_proj_b.astype(mx), "w_pc": w_proj_c.astype(mx),
        "w_o": w_o.astype(mx),
        "ln_g": ln_g[:, None, :], "ln_b": ln_b[:, None, :],
    }


def kernel(x, w_in, b_in, pool_w, pool_b, pool_scale, cmp_pos_k, cmp_pos_v, cmp_w1_k, cmp_w2_k, cmp_w1_v, cmp_w2_v, w_proj_a, w_proj_b, w_proj_c, w_o, ln_g, ln_b):
    depth = w_in.shape[0]
    s = x.shape[1]
    params = _prepare_params(w_in, b_in, pool_w, pool_b, pool_scale, cmp_pos_k, cmp_pos_v, cmp_w1_k, cmp_w2_k,
                             cmp_w1_v, cmp_w2_v, w_proj_a, w_proj_b, w_proj_c, w_o, ln_g, ln_b)
    consts = {"overlap_t": jnp.asarray(_overlap_matrix(s // CMP_STRIDE, s).T, _MXU_DTYPE)}
    alpha = (2 * depth) ** 0.25
    h32, h16 = x, x.astype(_MXU_DTYPE)
    for l in range(depth):
        h32, h16 = _layer(h32, h16, {k: v[l] for k, v in params.items()}, consts, alpha)
    return h32
```
